```python
import math
import jax
import jax.numpy as jnp
from jax import lax
import numpy as np

D_MODEL = 1024
BATCH = 2
SEQ = 8192
DEPTH = 2

N_META = 16
Q_BLOCK = 128
LN_EPS = 1e-5
RMS_EPS = 1e-6
NEG_INF = -1e30
DEEPNORM_ALPHA = (2.0 * DEPTH) ** 0.25
DEEPNORM_BETA = (8.0 * DEPTH) ** -0.25

LRU_WIDTH = 256
LRU_BLOCKS = 4
LRU_BLOCK_DIM = LRU_WIDTH // LRU_BLOCKS
LRU_CONV = 4
LRU_C = 8.0
FOX_HEADS = 4
FOX_HEAD_DIM = 64
FOX_WIDTH = FOX_HEADS * FOX_HEAD_DIM
MLA_HEADS = 4
MLA_Q_RANK = 256
MLA_KV_RANK = 128
MLA_NOPE_DIM = 64
MLA_ROPE_DIM = 32
MLA_V_DIM = 64
MLA_WIDTH = MLA_HEADS * MLA_V_DIM
ROPE_BASE = 10000.0
RWKV_HEADS = 4
RWKV_HEAD_DIM = 64
RWKV_WIDTH = RWKV_HEADS * RWKV_HEAD_DIM
RWKV_DECAY_RANK = 32
RWKV_AAA_RANK = 32
RWKV_GATE_RANK = 64
RWKV_GN_EPS = 64e-5
RWKV_SPLITS = (RWKV_WIDTH, RWKV_WIDTH, RWKV_WIDTH, RWKV_DECAY_RANK, RWKV_AAA_RANK, RWKV_GATE_RANK)
RWKV_IN_WIDTH = sum(RWKV_SPLITS)
N_BRANCH = 4
BRANCH_WIDTH = 256
D_FF = 2816
FFN_CONV = 3

IN_SPLITS = (LRU_WIDTH, LRU_WIDTH, FOX_WIDTH, FOX_WIDTH, FOX_WIDTH, FOX_HEADS,
             MLA_Q_RANK, MLA_KV_RANK, MLA_ROPE_DIM, RWKV_IN_WIDTH, N_BRANCH * D_MODEL)
D_IN = sum(IN_SPLITS)

kernel_name = 'hybrid_lru_fox_mla_rwkv7_block'


def split_last(z, sizes):
    return jnp.split(z, [int(s) for s in np.cumsum(sizes)[:-1]], axis=-1)


def layer_norm(x, g, b):
    xf = x.astype(jnp.float32)
    mu = jnp.mean(xf, -1, keepdims=True)
    var = jnp.mean(jnp.square(xf - mu), -1, keepdims=True)
    return ((xf - mu) * lax.rsqrt(var + LN_EPS) * g + b).astype(x.dtype)


def rms_norm(x, g):
    xf = x.astype(jnp.float32)
    return (xf * lax.rsqrt(jnp.mean(jnp.square(xf), -1, keepdims=True) + RMS_EPS) * g).astype(x.dtype)


def causal_depthwise_conv(x, w, b):
    K, C = w.shape
    y = lax.conv_general_dilated(x, w[:, None, :].astype(x.dtype), window_strides=(1,),
                                 padding=[(K - 1, 0)], dimension_numbers=('NWC', 'WIO', 'NWC'),
                                 feature_group_count=C)
    return y + b


def token_shift(z):
    return jnp.pad(z[:, :-1], ((0, 0), (1, 0), (0, 0)))


def rotary_tables(T, dim):
    inv = 1.0 / (ROPE_BASE ** (jnp.arange(0, dim, 2, dtype=jnp.float32) / dim))
    ang = jnp.arange(T, dtype=jnp.float32)[:, None] * inv[None, :]
    return jnp.cos(ang), jnp.sin(ang)


def apply_rotary(x, cos, sin):
    x1, x2 = jnp.split(x.astype(jnp.float32), 2, axis=-1)
    return jnp.concatenate([x1 * cos - x2 * sin, x2 * cos + x1 * sin], axis=-1).astype(x.dtype)


def block_causal_attention(q, k, v, cum_log_f=None):
    B, T, H, dk = q.shape
    pad_front = (-N_META) % Q_BLOCK
    pad_back = (-(pad_front + T)) % Q_BLOCK
    L = pad_front + T + pad_back
    nb = L // Q_BLOCK
    pad_t = lambda z: jnp.pad(z, ((0, 0), (pad_front, pad_back)) + ((0, 0),) * (z.ndim - 2))
    q, k, v = pad_t(q), pad_t(k), pad_t(v)
    kpos = jnp.arange(L)
    kvalid = kpos >= pad_front
    scale = 1.0 / math.sqrt(dk)
    xs = [jnp.arange(nb), jnp.moveaxis(q.reshape(B, nb, Q_BLOCK, H, dk), 1, 0)]
    F_bhl = None
    if cum_log_f is not None:
        F = pad_t(cum_log_f.astype(jnp.float32))
        F_bhl = jnp.swapaxes(F, 1, 2)
        xs.append(jnp.moveaxis(F.reshape(B, nb, Q_BLOCK, H), 1, 0))

    def one_block(args):
        i, q_i = args[0], args[1]
        qpos = i * Q_BLOCK + jnp.arange(Q_BLOCK)
        s = jnp.einsum('bqhd,bkhd->bhqk', q_i, k, preferred_element_type=jnp.float32) * scale
        if F_bhl is not None:
            f_i = jnp.swapaxes(args[2], 1, 2)
            s = s + (f_i[..., :, None] - F_bhl[:, :, None, :])
        mask = (kpos[None, :] <= qpos[:, None]) & kvalid[None, :]
        s = jnp.where(mask, s, NEG_INF)
        p = jax.nn.softmax(s, axis=-1).astype(v.dtype)
        return jnp.einsum('bhqk,bkhd->bqhd', p, v)

    out = lax.map(one_block, tuple(xs))
    out = jnp.moveaxis(out, 0, 1).reshape(B, L, H, v.shape[-1])
    return out[:, pad_front:pad_front + T]


def rglru_branch(y_in, x_in, conv_w, conv_b, w_rg, b_rg, w_ig, b_ig, lam):
    B, T, _ = x_in.shape
    f32 = jnp.float32
    xc = causal_depthwise_conv(x_in, conv_w, conv_b)
    xb = xc.reshape(B, T, LRU_BLOCKS, LRU_BLOCK_DIM)
    r = jax.nn.sigmoid((jnp.einsum('btnd,nde->btne', xb, w_rg).reshape(B, T, LRU_WIDTH) + b_rg).astype(f32))
    i = jax.nn.sigmoid((jnp.einsum('btnd,nde->btne', xb, w_ig).reshape(B, T, LRU_WIDTH) + b_ig).astype(f32))
    log_a = -LRU_C * r * jax.nn.softplus(-lam.astype(f32))
    a = jnp.exp(log_a)
    u = jnp.sqrt(-jnp.expm1(2.0 * log_a)) * (i * xc.astype(f32))

    def combine(left, right):
        a_l, h_l = left
        a_r, h_r = right
        return a_l * a_r, a_r * h_l + h_r

    _, h = lax.associative_scan(combine, (a, u), axis=1)
    return (jax.nn.gelu(y_in.astype(f32)) * h).astype(x_in.dtype)


def fox_branch(q, k, v, f_logit, b_f):
    B, T, _ = q.shape
    heads = lambda z: z.reshape(B, T, FOX_HEADS, FOX_HEAD_DIM)
    log_f = jax.nn.log_sigmoid(f_logit.astype(jnp.float32) + b_f)
    F = jnp.cumsum(log_f, axis=1)
    o = block_causal_attention(heads(q), heads(k), heads(v), F)
    return o.reshape(B, T, FOX_WIDTH)


def mla_branch(c_q, c_kv, k_r, q_norm_g, kv_norm_g, w_uq, w_ukv, cos, sin):
    B, T, _ = c_q.shape
    q = (rms_norm(c_q, q_norm_g) @ w_uq).reshape(B, T, MLA_HEADS, MLA_NOPE_DIM + MLA_ROPE_DIM)
    q_nope, q_rope = jnp.split(q, [MLA_NOPE_DIM], axis=-1)
    q_rope = apply_rotary(q_rope, cos[:, None, :], sin[:, None, :])
    kv = (rms_norm(c_kv, kv_norm_g) @ w_ukv).reshape(B, T, MLA_HEADS, MLA_NOPE_DIM + MLA_V_DIM)
    k_nope, v = jnp.split(kv, [MLA_NOPE_DIM], axis=-1)
    k_rope = apply_rotary(k_r, cos, sin)
    k = jnp.concatenate([k_nope, jnp.broadcast_to(k_rope[:, :, None, :], (B, T, MLA_HEADS, MLA_ROPE_DIM))], axis=-1)
    qf = jnp.concatenate([q_nope, q_rope], axis=-1)
    o = block_causal_attention(qf, k, v)
    return o.reshape(B, T, MLA_WIDTH)


def rwkv7_scan(r, w, k, v, a, b):
    B, T, H, N = r.shape

    def step(S, inp):
        r_t, w_t, k_t, v_t, a_t, b_t = inp
        sa = jnp.einsum('bhij,bhj->bhi', S, a_t)
        S = S * w_t[:, :, None, :] + sa[..., None] * b_t[:, :, None, :] + v_t[..., None] * k_t[:, :, None, :]
        return S, jnp.einsum('bhij,bhj->bhi', S, r_t)

    xs = tuple(jnp.moveaxis(t, 1, 0) for t in (r, w, k, v, a, b))
    _, ys = lax.scan(step, jnp.zeros((B, H, N, N), jnp.float32), xs)
    return jnp.moveaxis(ys, 0, 1)


def rwkv7_branch(p_in, mu, w0, w2, a0, a2, g2, k_k, k_a, r_k, gn_g, gn_b):
    B, T, _ = p_in.shape
    f32 = jnp.float32
    p = p_in + (token_shift(p_in) - p_in) * mu
    r, k, v, pw, pa, pg = split_last(p, RWKV_SPLITS)
    z = (w0 + jnp.tanh(pw) @ w2).astype(f32)
    decay = jnp.exp(-math.exp(-0.5) * jax.nn.sigmoid(z))
    a = jax.nn.sigmoid((a0 + pa @ a2).astype(f32))
    g = jax.nn.sigmoid(pg) @ g2
    heads = lambda t: t.reshape(B, T, RWKV_HEADS, RWKV_HEAD_DIM)
    kk = heads(k.astype(f32) * k_k)
    kk = kk * lax.rsqrt(jnp.maximum(jnp.sum(jnp.square(kk), -1, keepdims=True), 1e-24))
    kf = heads(k.astype(f32) * (1.0 + (a - 1.0) * k_a))
    rf, vf, af = heads(r.astype(f32)), heads(v.astype(f32)), heads(a)
    y = rwkv7_scan(rf, heads(decay), kf, vf, -kk, kk * af)
    y_mu = jnp.mean(y, -1, keepdims=True)
    y_var = jnp.mean(jnp.square(y - y_mu), -1, keepdims=True)
    y = (y - y_mu) * lax.rsqrt(y_var + RWKV_GN_EPS) * gn_g.reshape(RWKV_HEADS, RWKV_HEAD_DIM) \
        + gn_b.reshape(RWKV_HEADS, RWKV_HEAD_DIM)
    y = y + jnp.sum(rf * kf * r_k, -1, keepdims=True) * vf
    return (y.reshape(B, T, RWKV_WIDTH) * g).astype(p_in.dtype)


def conv_glu_ffn(x, w_up, conv_w, conv_b, w_down):
    hdn = causal_depthwise_conv(x @ w_up, conv_w, conv_b)
    gate, val = jnp.split(hdn, 2, axis=-1)
    return (jax.nn.silu(gate) * val) @ w_down


def setup_inputs(seed: int = 0) -> dict:
    key = jax.random.key(seed)
    ks = list(jax.random.split(key, 48))
    f32 = jnp.float32
    L = DEPTH

    def nrm(shape, scale):
        return jax.random.normal(ks.pop(), shape, f32) * scale

    def gain(shape):
        return 1.0 + nrm(shape, 0.02)

    u = jax.random.uniform(ks.pop(), (L, LRU_WIDTH), f32, 0.9, 0.999)
    a_base = u ** (1.0 / LRU_C)
    lru_lambda = jnp.log(a_base) - jnp.log1p(-a_base)
    rwkv_mu = jax.random.uniform(ks.pop(), (L, RWKV_IN_WIDTH), f32)
    return {
        'x': nrm((BATCH, SEQ, D_MODEL), 1.0),
        'meta_tokens': nrm((N_META, D_MODEL), 1.0),
        'ln_in_g': gain((D_MODEL,)),
        'ln_in_b': nrm((D_MODEL,), 0.02),
        'w_in': nrm((L, D_MODEL, D_IN), D_MODEL ** -0.5),
        'w_branch': nrm((L, N_BRANCH, BRANCH_WIDTH, D_MODEL), BRANCH_WIDTH ** -0.5),
        'w_out': nrm((L, D_MODEL, D_MODEL), DEEPNORM_BETA * D_MODEL ** -0.5),
        'ln_mix_g': gain((L, D_MODEL)),
        'ln_mix_b': nrm((L, D_MODEL), 0.02),
        'lru_conv_w': nrm((L, LRU_CONV, LRU_WIDTH), LRU_CONV ** -0.5),
        'lru_conv_b': nrm((L, LRU_WIDTH), 0.02),
        'lru_w_rg': nrm((L, LRU_BLOCKS, LRU_BLOCK_DIM, LRU_BLOCK_DIM), LRU_BLOCK_DIM ** -0.5),
        'lru_b_rg': nrm((L, LRU_WIDTH), 0.02),
        'lru_w_ig': nrm((L, LRU_BLOCKS, LRU_BLOCK_DIM, LRU_BLOCK_DIM), LRU_BLOCK_DIM ** -0.5),
        'lru_b_ig': nrm((L, LRU_WIDTH), 0.02),
        'lru_lambda': lru_lambda,
        'fox_b_f': nrm((L, FOX_HEADS), 0.02),
        'mla_q_norm_g': gain((L, MLA_Q_RANK)),
        'mla_kv_norm_g': gain((L, MLA_KV_RANK)),
        'mla_w_uq': nrm((L, MLA_Q_RANK, MLA_HEADS * (MLA_NOPE_DIM + MLA_ROPE_DIM)), MLA_Q_RANK ** -0.5),
        'mla_w_ukv': nrm((L, MLA_KV_RANK, MLA_HEADS * (MLA_NOPE_DIM + MLA_V_DIM)), MLA_KV_RANK ** -0.5),
        'rwkv_mu': rwkv_mu,
        'rwkv_w0': nrm((L, RWKV_WIDTH), 0.5),
        'rwkv_w2': nrm((L, RWKV_DECAY_RANK, RWKV_WIDTH), 0.1 * RWKV_DECAY_RANK ** -0.5),
        'rwkv_a0': nrm((L, RWKV_WIDTH), 0.1),
        'rwkv_a2': nrm((L, RWKV_AAA_RANK, RWKV_WIDTH), 0.1 * RWKV_AAA_RANK ** -0.5),
        'rwkv_g2': nrm((L, RWKV_GATE_RANK, RWKV_WIDTH), RWKV_GATE_RANK ** -0.5),
        'rwkv_k_k': 0.85 + nrm((L, RWKV_WIDTH), 0.02),
        'rwkv_k_a': gain((L, RWKV_WIDTH)),
        'rwkv_r_k': nrm((L, RWKV_HEADS, RWKV_HEAD_DIM), 0.1),
        'rwkv_gn_g': gain((L, RWKV_WIDTH)),
        'rwkv_gn_b': nrm((L, RWKV_WIDTH), 0.02),
        'ffn_w_up': nrm((L, D_MODEL, 2 * D_FF), D_MODEL ** -0.5),
        'ffn_conv_w': nrm((L, FFN_CONV, 2 * D_FF), FFN_CONV ** -0.5),
        'ffn_conv_b': nrm((L, 2 * D_FF), 0.02),
        'ffn_w_down': nrm((L, D_FF, D_MODEL), DEEPNORM_BETA * D_FF ** -0.5),
        'ln_ffn_g': gain((L, D_MODEL)),
        'ln_ffn_b': nrm((L, D_MODEL), 0.02),
    }


def reference(x, meta_tokens, ln_in_g, ln_in_b, w_in, w_branch, w_out, ln_mix_g, ln_mix_b,
              lru_conv_w, lru_conv_b, lru_w_rg, lru_b_rg, lru_w_ig, lru_b_ig, lru_lambda,
              fox_b_f, mla_q_norm_g, mla_kv_norm_g, mla_w_uq, mla_w_ukv,
              rwkv_mu, rwkv_w0, rwkv_w2, rwkv_a0, rwkv_a2, rwkv_g2, rwkv_k_k, rwkv_k_a, rwkv_r_k,
              rwkv_gn_g, rwkv_gn_b, ffn_w_up, ffn_conv_w, ffn_conv_b, ffn_w_down, ln_ffn_g, ln_ffn_b):
    B = x.shape[0]
    meta = jnp.broadcast_to(meta_tokens.astype(x.dtype)[None], (B, N_META, D_MODEL))
    h = layer_norm(jnp.concatenate([meta, x], axis=1), ln_in_g, ln_in_b)
    T = h.shape[1]
    cos, sin = rotary_tables(T, MLA_ROPE_DIM)
    for l in range(DEPTH):
        (lru_y, lru_x, fox_q, fox_k, fox_v, fox_f, mla_cq, mla_ckv, mla_kr,
         rwkv_in, gate_logits) = split_last(h @ w_in[l], IN_SPLITS)
        o_a = rglru_branch(lru_y, lru_x, lru_conv_w[l], lru_conv_b[l], lru_w_rg[l], lru_b_rg[l],
                           lru_w_ig[l], lru_b_ig[l], lru_lambda[l])
        o_b = fox_branch(fox_q, fox_k, fox_v, fox_f, fox_b_f[l])
        o_c = mla_branch(mla_cq, mla_ckv, mla_kr, mla_q_norm_g[l], mla_kv_norm_g[l],
                         mla_w_uq[l], mla_w_ukv[l], cos, sin)
        o_d = rwkv7_branch(rwkv_in, rwkv_mu[l], rwkv_w0[l], rwkv_w2[l], rwkv_a0[l], rwkv_a2[l],
                           rwkv_g2[l], rwkv_k_k[l], rwkv_k_a[l], rwkv_r_k[l], rwkv_gn_g[l], rwkv_gn_b[l])
        branches = jnp.stack([o_a, o_b, o_c, o_d], axis=2)
        proj = jnp.einsum('btnc,ncd->btnd', branches, w_branch[l])
        gates = jax.nn.sigmoid(gate_logits.reshape(B, T, N_BRANCH, D_MODEL))
        mixed = jnp.sum(gates * proj, axis=2) @ w_out[l]
        h = layer_norm(DEEPNORM_ALPHA * h + mixed, ln_mix_g[l], ln_mix_b[l])
        ffn = conv_glu_ffn(h, ffn_w_up[l], ffn_conv_w[l], ffn_conv_b[l], ffn_w_down[l])
        h = layer_norm(DEEPNORM_ALPHA * h + ffn, ln_ffn_g[l], ln_ffn_b[l])
    return h[:, N_META:]
```

```python
import functools
import math

import jax
import jax.numpy as jnp
import numpy as np
from jax import lax
from jax.experimental import pallas as pl
from jax.experimental.pallas import tpu as pltpu

F32 = jnp.float32
BF16 = jnp.bfloat16

D_MODEL = 1024
N_META = 16
LN_EPS = 1e-5
RMS_EPS = 1e-6
NEG_BIG = -1e30
LOG2E = 1.4426950408889634

LRU_WIDTH = 256
LRU_BLOCKS = 4
LRU_CONV = 4
LRU_C = 8.0
HEADS = 4
HEAD_DIM = 64
WIDTH = HEADS * HEAD_DIM
MLA_Q_RANK = 256
MLA_KV_RANK = 128
MLA_NOPE = 64
MLA_ROPE = 32
ROPE_BASE = 10000.0
RWKV_LORA = 128
RWKV_IN = 3 * WIDTH + RWKV_LORA
RWKV_GN_EPS = 64e-5
RWKV_DECAY_SCALE = math.exp(-0.5)
N_BRANCH = 4
D_FF = 2816
FFN_CONV = 3

SUBLANES = 8
LANES = 128
MXU_DIM = 256
ROW_TILE = 256
RWKV_CHUNK = 64
RWKV_SUB = 16
FFN_CHUNK = 256
VMEM_LIMIT = 56 * 1024 * 1024


def _cparams(*sem):
    return pltpu.CompilerParams(dimension_semantics=sem, vmem_limit_bytes=VMEM_LIMIT)


def _whole(shape):
    nd = len(shape)
    return pl.BlockSpec(shape, lambda *_: (0,) * nd)


def _rows(tile, width):
    return pl.BlockSpec((None, tile, width), lambda b, t: (b, t, 0))


def _iota(shape, dim):
    return lax.broadcasted_iota(jnp.int32, shape, dim)


def _dot(a, b):
    return jnp.dot(a, b, preferred_element_type=F32)


def _dot_nt(a, b):
    return lax.dot_general(a, b, (((1,), (1,)), ((), ())), preferred_element_type=F32)


def _dot_tn(a, b):
    return lax.dot_general(a, b, (((0,), (0,)), ((), ())), preferred_element_type=F32)


def _split(x, terms):
    out = []
    for _ in range(terms - 1):
        hi = x.astype(BF16)
        out.append(hi)
        x = x - hi.astype(F32)
    out.append(x.astype(BF16))
    return out


def _mdot(dot, a_terms, b_terms):
    order = max(len(a_terms), len(b_terms))
    acc = None
    for i, a in enumerate(a_terms):
        for j, b in enumerate(b_terms):
            if i + j < order:
                p = dot(a, b)
                acc = p if acc is None else acc + p
    return acc


def _sigmoid(x):
    return 1.0 / (1.0 + jnp.exp(-x))


def _softplus(x):
    return jnp.maximum(x, 0.0) + jnp.log(1.0 + jnp.exp(-jnp.abs(x)))


def _gelu_tanh(x):
    return 0.5 * x * (1.0 + jnp.tanh(math.sqrt(2.0 / math.pi) * (x + 0.044715 * (x * x * x))))


def _layer_norm(x, g, b):
    mu = jnp.mean(x, -1, keepdims=True)
    xc = x - mu
    var = jnp.mean(xc * xc, -1, keepdims=True)
    return xc * lax.rsqrt(var + LN_EPS) * g + b


def _shift_rows(x, d, carry):
    rows = _iota(x.shape, 0)
    out = pltpu.roll(x, d, 0)
    for r in range(d):
        src = SUBLANES - d + r
        out = jnp.where(rows == r, carry[src:src + 1, :], out)
    return out


def _cumsum_rows(x):
    rows = _iota(x.shape, 0)
    d = 1
    while d < x.shape[0]:
        x = x + jnp.where(rows >= d, pltpu.roll(x, d, 0), 0.0)
        d *= 2
    return x


def _linear_scan_rows(a, u):
    rows = _iota(a.shape, 0)
    d = 1
    while d < a.shape[0]:
        keep = rows >= d
        u = u + a * jnp.where(keep, pltpu.roll(u, d, 0), 0.0)
        a = a * jnp.where(keep, pltpu.roll(a, d, 0), 1.0)
        d *= 2
    return a, u


def _head_mask(shape, lane_dim, head):
    lane = _iota(shape, lane_dim)
    return (lane >= head * HEAD_DIM) & (lane < (head + 1) * HEAD_DIM)


def _stack_heads(x):
    return jnp.concatenate(
        [jnp.where(_head_mask(x.shape, 1, h), x, jnp.zeros_like(x)) for h in range(HEADS)], axis=0)


def _unstack_heads(x):
    r = x.shape[0] // HEADS
    out = x[0:r]
    for h in range(1, HEADS):
        out = out + x[h * r:(h + 1) * r]
    return out


def _head_sum(x, ones_bd):
    return _mdot(_dot, _split(x, 2), [ones_bd])


def _ln_in_kernel(x_ref, g_ref, b_ref, o32_ref, o16_ref):
    y = _layer_norm(x_ref[...], g_ref[...], b_ref[...])
    o32_ref[...] = y
    o16_ref[...] = y.astype(BF16)


def _ln_in(x, g, b):
    bsz, lp, d = x.shape
    return pl.pallas_call(
        _ln_in_kernel,
        grid=(bsz, lp // ROW_TILE),
        in_specs=[_rows(ROW_TILE, d), _whole((1, d)), _whole((1, d))],
        out_specs=[_rows(ROW_TILE, d), _rows(ROW_TILE, d)],
        out_shape=[jax.ShapeDtypeStruct(x.shape, F32), jax.ShapeDtypeStruct(x.shape, BF16)],
        compiler_params=_cparams("parallel", "parallel"),
        name="ln_in",
    )(x, g.reshape(1, d), b.reshape(1, d))


def _lru_kernel(h_ref, w_ref, cw_ref, cb_ref, wrg_ref, brg_ref, wig_ref, big_ref, lam_ref,
                o_ref, xcarry_ref, hcarry_ref):
    @pl.when(pl.program_id(1) == 0)
    def _():
        xcarry_ref[...] = jnp.zeros_like(xcarry_ref)
        hcarry_ref[...] = jnp.zeros_like(hcarry_ref)

    yx = _dot(h_ref[...], w_ref[...])
    y = yx[:, :LRU_WIDTH]
    x = yx[:, LRU_WIDTH:]
    xcarry = xcarry_ref[...]
    xc = cw_ref[LRU_CONV - 1:LRU_CONV, :] * x + cb_ref[...]
    for d in range(1, LRU_CONV):
        k = LRU_CONV - 1 - d
        xc = xc + cw_ref[k:k + 1, :] * _shift_rows(x, d, xcarry)
    xcarry_ref[...] = x[ROW_TILE - SUBLANES:, :]

    xc16 = xc.astype(BF16)
    r = _sigmoid(_dot(xc16, wrg_ref[...]) + brg_ref[...])
    i = _sigmoid(_dot(xc16, wig_ref[...]) + big_ref[...])
    log_a = (-LRU_C) * r * _softplus(-lam_ref[...])
    a = jnp.exp(log_a)
    th = jnp.tanh(log_a)
    u = jnp.sqrt(-2.0 * th / (1.0 - th)) * (i * xc)
    a_run, h = _linear_scan_rows(a, u)
    h = h + a_run * hcarry_ref[SUBLANES - 1:SUBLANES, :]
    hcarry_ref[...] = h[ROW_TILE - SUBLANES:, :]
    o_ref[...] = (_gelu_tanh(y) * h).astype(BF16)


def _block_diag(w):
    n, d, e = w.shape
    eye = jnp.eye(n, dtype=w.dtype)
    return (eye[:, None, :, None] * w[:, :, None, :]).reshape(n * d, n * e)


def _lru_branch(h16, w_yx, conv_w, conv_b, w_rg, b_rg, w_ig, b_ig, lam):
    bsz, lp, d = h16.shape
    row = lambda v: v.reshape(1, LRU_WIDTH)
    return pl.pallas_call(
        _lru_kernel,
        grid=(bsz, lp // ROW_TILE),
        in_specs=[_rows(ROW_TILE, d), _whole((d, 2 * LRU_WIDTH)),
                  _whole((LRU_CONV, LRU_WIDTH)), _whole((1, LRU_WIDTH)),
                  _whole((LRU_WIDTH, LRU_WIDTH)), _whole((1, LRU_WIDTH)),
                  _whole((LRU_WIDTH, LRU_WIDTH)), _whole((1, LRU_WIDTH)),
                  _whole((1, LRU_WIDTH))],
        out_specs=_rows(ROW_TILE, LRU_WIDTH),
        out_shape=jax.ShapeDtypeStruct((bsz, lp, LRU_WIDTH), BF16),
        scratch_shapes=[pltpu.VMEM((SUBLANES, LRU_WIDTH), F32), pltpu.VMEM((SUBLANES, LRU_WIDTH), F32)],
        compiler_params=_cparams("parallel", "arbitrary"),
        name="lru",
    )(h16, w_yx.astype(BF16), conv_w, row(conv_b), _block_diag(w_rg).astype(BF16), row(b_rg),
      _block_diag(w_ig).astype(BF16), row(b_ig), row(lam))


def _fox_proj_kernel(h_ref, w_ref, bf_ref, q_ref, k_ref, v_ref, f_ref, fcarry_ref):
    @pl.when(pl.program_id(1) == 0)
    def _():
        fcarry_ref[...] = jnp.zeros_like(fcarry_ref)

    z = _dot(h_ref[...], w_ref[...])
    q_ref[...] = z[:, 0:WIDTH].astype(BF16)
    k_ref[...] = z[:, WIDTH:2 * WIDTH].astype(BF16)
    v_ref[...] = z[:, 2 * WIDTH:3 * WIDTH].astype(BF16)
    log_f = -_softplus(-(z[:, 3 * WIDTH:] + bf_ref[...]))
    f = _cumsum_rows(log_f) + fcarry_ref[SUBLANES - 1:SUBLANES, :]
    fcarry_ref[...] = f[ROW_TILE - SUBLANES:, :]
    f_ref[...] = f


def _fox_proj(h16, w_qkvf, b_f):
    bsz, lp, d = h16.shape
    bf = jnp.zeros((1, LANES), F32).at[0, :HEADS].set(b_f)
    act = jax.ShapeDtypeStruct((bsz, lp, WIDTH), BF16)
    return pl.pallas_call(
        _fox_proj_kernel,
        grid=(bsz, lp // ROW_TILE),
        in_specs=[_rows(ROW_TILE, d), _whole((d, 3 * WIDTH + LANES)), _whole((1, LANES))],
        out_specs=[_rows(ROW_TILE, WIDTH)] * 3 + [_rows(ROW_TILE, LANES)],
        out_shape=[act, act, act, jax.ShapeDtypeStruct((bsz, lp, LANES), F32)],
        scratch_shapes=[pltpu.VMEM((SUBLANES, LANES), F32)],
        compiler_params=_cparams("parallel", "arbitrary"),
        name="fox_proj",
    )(h16, w_qkvf.astype(BF16), bf)


def _rms_norm(x, g):
    return x * lax.rsqrt(jnp.mean(x * x, -1, keepdims=True) + RMS_EPS) * g


def _mla_proj_kernel(h_ref, win_ref, qg_ref, kvg_ref, wq_ref, wkv_ref, cos_ref, sin_ref,
                     qn_ref, qr_ref, k_ref, v_ref):
    z = _dot(h_ref[...], win_ref[...])
    c_q = z[:, 0:MLA_Q_RANK]
    c_kv = z[:, MLA_Q_RANK:MLA_Q_RANK + MLA_KV_RANK]
    k_r = z[:, MLA_Q_RANK + MLA_KV_RANK:MLA_Q_RANK + MLA_KV_RANK + LANES]
    k_r_swapped = z[:, MLA_Q_RANK + MLA_KV_RANK + LANES:]
    cos = cos_ref[...]
    sin = sin_ref[...]

    q = _dot(_rms_norm(c_q, qg_ref[...]).astype(BF16), wq_ref[...])
    qn_ref[...] = q[:, 0:WIDTH].astype(BF16)
    rope_w = HEADS * LANES
    for h in range(HEADS):
        plain = q[:, WIDTH + h * LANES:WIDTH + (h + 1) * LANES]
        swapped = q[:, WIDTH + rope_w + h * LANES:WIDTH + rope_w + (h + 1) * LANES]
        qr_ref[:, h * LANES:(h + 1) * LANES] = (plain * cos + swapped * sin).astype(BF16)

    kv = _dot(_rms_norm(c_kv, kvg_ref[...]).astype(BF16), wkv_ref[...])
    k_ref[:, 0:WIDTH] = kv[:, 0:WIDTH].astype(BF16)
    k_ref[:, WIDTH:] = (k_r * cos + k_r_swapped * sin).astype(BF16)
    v_ref[...] = kv[:, WIDTH:].astype(BF16)


def _swap_halves(w):
    half = w.shape[-1] // 2
    return jnp.concatenate([w[..., half:], w[..., :half]], axis=-1)


def _pad_cols(w, width):
    return jnp.pad(w, ((0, 0), (0, width - w.shape[1])))


def _mla_proj(h16, w_cq, w_ckv, w_kr, q_norm_g, kv_norm_g, w_uq, w_ukv, cos_t, sin_t):
    bsz, lp, d = h16.shape
    w_in = jnp.concatenate([w_cq, w_ckv, _pad_cols(w_kr, LANES), _pad_cols(_swap_halves(w_kr), LANES)], axis=1)
    uq = w_uq.reshape(MLA_Q_RANK, HEADS, MLA_NOPE + MLA_ROPE)
    q_nope = uq[:, :, :MLA_NOPE].reshape(MLA_Q_RANK, WIDTH)
    q_rope = uq[:, :, MLA_NOPE:]
    lane_pad = lambda r: jnp.pad(r, ((0, 0), (0, 0), (0, LANES - MLA_ROPE))).reshape(MLA_Q_RANK, HEADS * LANES)
    wq = jnp.concatenate([q_nope, lane_pad(q_rope), lane_pad(_swap_halves(q_rope))], axis=1)
    ukv = w_ukv.reshape(MLA_KV_RANK, HEADS, MLA_NOPE + HEAD_DIM)
    wkv = jnp.concatenate([ukv[:, :, :MLA_NOPE].reshape(MLA_KV_RANK, WIDTH),
                           ukv[:, :, MLA_NOPE:].reshape(MLA_KV_RANK, WIDTH)], axis=1)
    table = pl.BlockSpec((ROW_TILE, LANES), lambda b, t: (t, 0))
    act = lambda w: jax.ShapeDtypeStruct((bsz, lp, w), BF16)
    return pl.pallas_call(
        _mla_proj_kernel,
        grid=(bsz, lp // ROW_TILE),
        in_specs=[_rows(ROW_TILE, d), _whole(w_in.shape), _whole((1, MLA_Q_RANK)), _whole((1, MLA_KV_RANK)),
                  _whole(wq.shape), _whole(wkv.shape), table, table],
        out_specs=[_rows(ROW_TILE, WIDTH), _rows(ROW_TILE, HEADS * LANES),
                   _rows(ROW_TILE, WIDTH + LANES), _rows(ROW_TILE, WIDTH)],
        out_shape=[act(WIDTH), act(HEADS * LANES), act(WIDTH + LANES), act(WIDTH)],
        compiler_params=_cparams("parallel", "parallel"),
        name="mla_proj",
    )(h16, w_in.astype(BF16), q_norm_g.reshape(1, -1), kv_norm_g.reshape(1, -1),
      wq.astype(BF16), wkv.astype(BF16), cos_t, sin_t)


def _rotary_tables(lp):
    inv = 1.0 / (ROPE_BASE ** (jnp.arange(0, MLA_ROPE, 2, dtype=F32) / MLA_ROPE))
    ang = jnp.arange(lp, dtype=F32)[:, None] * inv[None, :]
    cos, sin = jnp.cos(ang), jnp.sin(ang)
    pad = jnp.zeros((lp, LANES - MLA_ROPE), F32)
    return (jnp.concatenate([cos, cos, pad], axis=1), jnp.concatenate([-sin, sin, pad], axis=1))


def _attn_kernel(*refs, scale, has_rope, has_bias):
    it = iter(refs)
    q_ref = next(it)
    qr_ref = next(it) if has_rope else None
    k_ref = next(it)
    v_ref = next(it)
    f_ref = next(it) if has_bias else None
    o_ref = next(it)
    m_ref, l_ref, acc_ref = next(it), next(it), next(it)

    tq = q_ref.shape[0]
    qi = pl.program_id(1)
    q = q_ref[...]
    parts = []
    for h in range(HEADS):
        qh = jnp.where(_head_mask(q.shape, 1, h), q, jnp.zeros_like(q))
        if has_rope:
            qh = jnp.concatenate([qh, qr_ref[:, h * LANES:(h + 1) * LANES]], axis=1)
        parts.append(qh)
    q_stack = jnp.concatenate(parts, axis=0)

    m_ref[...] = jnp.full(m_ref.shape, NEG_BIG, F32)
    l_ref[...] = jnp.zeros_like(l_ref)
    acc_ref[...] = jnp.zeros_like(acc_ref)
    c = scale * LOG2E
    if has_bias:
        f_q0 = f_ref[:, pl.ds(pl.multiple_of(qi * tq, tq), LANES)][:, 0:1]

    def block(j, masked):
        start = pl.multiple_of(j * tq, tq)
        kb = k_ref[pl.ds(start, tq), :]
        vb = v_ref[pl.ds(start, tq), :]
        s = _dot_nt(q_stack, kb) * c
        if has_bias:
            bias = (f_q0 - f_ref[:, pl.ds(start, tq)]) * LOG2E
        if masked:
            keep = _iota((tq, tq), 0) >= _iota((tq, tq), 1)
        p_parts = []
        for h in range(HEADS):
            rows = slice(h * tq, (h + 1) * tq)
            sh = s[rows]
            if has_bias:
                sh = sh + bias[h:h + 1, :]
            if masked:
                sh = jnp.where(keep, sh, NEG_BIG)
            m_old = m_ref[rows]
            m_new = jnp.maximum(m_old, jnp.max(sh, axis=1, keepdims=True))
            p = jnp.exp2(sh - m_new)
            alpha = jnp.exp2(m_old - m_new)
            l_ref[rows] = alpha * l_ref[rows] + jnp.sum(p, axis=1, keepdims=True)
            m_ref[rows] = m_new
            acc_ref[rows] = acc_ref[rows] * alpha
            p_parts.append(p.astype(BF16))
        acc_ref[...] += _dot(jnp.concatenate(p_parts, axis=0), vb)

    def body(j, carry):
        block(j, False)
        return carry

    lax.fori_loop(0, qi, body, 0)
    block(qi, True)

    out = jnp.zeros((tq, WIDTH), F32)
    for h in range(HEADS):
        rows = slice(h * tq, (h + 1) * tq)
        oh = acc_ref[rows] / l_ref[rows]
        out = jnp.where(_head_mask(out.shape, 1, h), oh, out)
    o_ref[...] = out.astype(BF16)


def _attention(q, k, v, scale, q_rope=None, f_rows=None):
    bsz, lp, _ = q.shape
    tq = ROW_TILE
    dk = k.shape[-1]
    resident = lambda w: pl.BlockSpec((None, lp, w), lambda b, t: (b, 0, 0))
    in_specs = [_rows(tq, WIDTH)]
    args = [q]
    if q_rope is not None:
        in_specs.append(_rows(tq, HEADS * LANES))
        args.append(q_rope)
    in_specs += [resident(dk), resident(WIDTH)]
    args += [k, v]
    if f_rows is not None:
        in_specs.append(pl.BlockSpec((None, SUBLANES, lp), lambda b, t: (b, 0, 0)))
        args.append(f_rows)
    kern = functools.partial(_attn_kernel, scale=scale, has_rope=q_rope is not None,
                             has_bias=f_rows is not None)
    return pl.pallas_call(
        kern,
        grid=(bsz, lp // tq),
        in_specs=in_specs,
        out_specs=_rows(tq, WIDTH),
        out_shape=jax.ShapeDtypeStruct((bsz, lp, WIDTH), BF16),
        scratch_shapes=[pltpu.VMEM((HEADS * tq, 1), F32), pltpu.VMEM((HEADS * tq, 1), F32),
                        pltpu.VMEM((HEADS * tq, WIDTH), F32)],
        compiler_params=_cparams("parallel", "arbitrary"),
        name="attn_bias" if f_rows is not None else "attn_rope",
    )(*args)


def _unit_lower_inverse(m, sub_blk):
    n = m.shape[0]
    eye = (_iota((n, n), 0) == _iota((n, n), 1)).astype(F32)
    md = jnp.where(sub_blk, m, 0.0)
    mo = m - md
    mm = lambda a, b: _mdot(_dot, _split(a, 2), _split(b, 2))
    x = eye + md
    p = md
    for _ in range(int(math.log2(RWKV_SUB)) - 1):
        p = mm(p, p)
        x = mm(x, eye + p)
    pm = mm(x, mo)
    pm2 = mm(pm, pm)
    series = eye + pm + pm2 + mm(pm, pm2)
    return mm(series, x)


def _rwkv_kernel(h_ref, w_ref, mu_ref, w0_ref, w2_ref, a0_ref, a2_ref, g2_ref, kk_ref, ka_ref,
                 rk_ref, gng_ref, gnb_ref, ones_ref, o_ref, pcarry_ref, state_ref, y_ref):
    @pl.when(pl.program_id(1) == 0)
    def _():
        pcarry_ref[...] = jnp.zeros_like(pcarry_ref)
        state_ref[...] = jnp.zeros_like(state_ref)

    p_in = _dot(h_ref[...], w_ref[...])
    prev = _shift_rows(p_in, 1, pcarry_ref[...])
    pcarry_ref[...] = p_in[ROW_TILE - SUBLANES:, :]
    p = p_in + (prev - p_in) * mu_ref[...]
    r = p[:, 0:WIDTH]
    k = p[:, WIDTH:2 * WIDTH]
    v = p[:, 2 * WIDTH:3 * WIDTH]
    lora = p[:, 3 * WIDTH:]
    ones_bd = ones_ref[...]

    z = w0_ref[...] + _dot(jnp.tanh(lora).astype(BF16), w2_ref[...])
    log_w = (-RWKV_DECAY_SCALE) * _sigmoid(z)
    a_gate = _sigmoid(a0_ref[...] + _dot(lora.astype(BF16), a2_ref[...]))
    gate = _dot(_sigmoid(lora).astype(BF16), g2_ref[...])
    kk = k * kk_ref[...]
    kk = kk * lax.rsqrt(jnp.maximum(_head_sum(kk * kk, ones_bd), 1e-24))
    kf = k * (1.0 + (a_gate - 1.0) * ka_ref[...])
    a_vec = -kk
    b_vec = kk * a_gate

    n = HEADS * RWKV_CHUNK
    ri, ci = _iota((n, n), 0), _iota((n, n), 1)
    strict = ri > ci
    incl = ri >= ci
    sub_blk = (ri // RWKV_SUB) == (ci // RWKV_SUB)
    diag = ri == ci
    mid = RWKV_CHUNK // 2 - 1
    sp = lambda t: _split(t, 2)
    mm = lambda a, b: _mdot(_dot, a, b)
    mm_nt = lambda a, b: _mdot(_dot_nt, a, b)
    mm_tn = lambda a, b: _mdot(_dot_tn, a, b)

    for c in range(ROW_TILE // RWKV_CHUNK):
        rows = slice(c * RWKV_CHUNK, (c + 1) * RWKV_CHUNK)
        lw = log_w[rows]
        g_in = _cumsum_rows(lw)
        g_ex = g_in - lw
        g_mid = g_in[mid:mid + 1, :]
        g_end = g_in[RWKV_CHUNK - 1:RWKV_CHUNK, :]
        decay_out = jnp.exp(g_mid - g_in)
        a_c, b_c, k_c, r_c, v_c = a_vec[rows], b_vec[rows], kf[rows], r[rows], v[rows]
        a_t = sp(_stack_heads(a_c * jnp.exp(g_ex - g_mid)))
        r_t = sp(_stack_heads(r_c * jnp.exp(g_in - g_mid)))
        b_t = sp(_stack_heads(b_c * decay_out))
        k_t = sp(_stack_heads(k_c * decay_out))
        a_0 = sp(_stack_heads(a_c * jnp.exp(g_ex)))
        r_0 = _stack_heads(r_c * jnp.exp(g_in))
        to_end = jnp.exp(g_end - g_in)
        b_e = sp(_stack_heads(b_c * to_end))
        k_e = sp(_stack_heads(k_c * to_end))
        v_s = sp(_stack_heads(v_c))

        m_ab = jnp.where(strict, mm_nt(a_t, b_t), 0.0)
        m_ak = sp(jnp.where(strict, mm_nt(a_t, k_t), 0.0))
        m_rb = sp(jnp.where(incl, mm_nt(r_t, b_t), 0.0))
        m_rk = sp(jnp.where(incl, mm_nt(r_t, k_t), 0.0))
        t_inv = sp(_unit_lower_inverse(m_ab, sub_blk))

        w_a = sp(mm(t_inv, a_0))
        u_v = sp(mm(t_inv, sp(mm(m_ak, v_s))))
        r_eff = sp(r_0 + mm(m_rb, w_a))
        y_v = mm(m_rb, u_v) + mm(m_rk, v_s)
        g_mat = jnp.where(diag, jnp.exp(g_end), 0.0) + mm_tn(b_e, w_a)
        d_mat = mm_tn(b_e, u_v) + mm_tn(k_e, v_s)

        state = sp(state_ref[...])
        y_ref[rows, :] = _unstack_heads(mm(r_eff, state) + y_v)
        state_ref[...] = mm(sp(g_mat), state) + d_mat

    y = y_ref[...]
    inv_n = 1.0 / HEAD_DIM
    y_mu = _head_sum(y, ones_bd) * inv_n
    yc = y - y_mu
    y_var = _head_sum(yc * yc, ones_bd) * inv_n
    y = yc * lax.rsqrt(y_var + RWKV_GN_EPS) * gng_ref[...] + gnb_ref[...]
    y = y + _head_sum(r * kf * rk_ref[...], ones_bd) * v
    o_ref[...] = (y * gate).astype(BF16)


def _rwkv_branch(h16, w_p, mu, w0, w2, a0, a2, g2, k_k, k_a, r_k, gn_g, gn_b):
    bsz, lp, d = h16.shape
    row = lambda t: t.reshape(1, -1)
    lora_rows = lambda w, lo: jnp.zeros((RWKV_LORA, WIDTH), F32).at[lo:lo + w.shape[0]].set(w).astype(BF16)
    ones_bd = _block_diag(jnp.ones((HEADS, HEAD_DIM, HEAD_DIM), F32)).astype(BF16)
    params = [w_p.astype(BF16), row(mu), row(w0), lora_rows(w2, 0), row(a0), lora_rows(a2, 32),
              lora_rows(g2, 64), row(k_k), row(k_a), row(r_k), row(gn_g), row(gn_b), ones_bd]
    return pl.pallas_call(
        _rwkv_kernel,
        grid=(bsz, lp // ROW_TILE),
        in_specs=[_rows(ROW_TILE, d)] + [_whole(t.shape) for t in params],
        out_specs=_rows(ROW_TILE, WIDTH),
        out_shape=jax.ShapeDtypeStruct((bsz, lp, WIDTH), BF16),
        scratch_shapes=[pltpu.VMEM((SUBLANES, RWKV_IN), F32), pltpu.VMEM((WIDTH, WIDTH), F32),
                        pltpu.VMEM((ROW_TILE, WIDTH), F32)],
        compiler_params=_cparams("parallel", "arbitrary"),
        name="rwkv",
    )(h16, *params)


def _merge_kernel(h32_ref, h16_ref, oa_ref, ob_ref, oc_ref, od_ref, wg_ref, wb_ref, wo_ref, g_ref, b_ref,
                  o32_ref, o16_ref, *, alpha):
    h16 = h16_ref[...]
    acc = None
    for n, o_ref in enumerate((oa_ref, ob_ref, oc_ref, od_ref)):
        gate = _sigmoid(_dot(h16, wg_ref[:, n * D_MODEL:(n + 1) * D_MODEL]))
        term = gate * _dot(o_ref[...], wb_ref[n])
        acc = term if acc is None else acc + term
    mixed = _dot(acc.astype(BF16), wo_ref[...])
    y = _layer_norm(alpha * h32_ref[...] + mixed, g_ref[...], b_ref[...])
    o32_ref[...] = y
    o16_ref[...] = y.astype(BF16)


def _merge(h32, h16, branches, w_gate, w_branch, w_out, ln_g, ln_b, alpha):
    bsz, lp, d = h32.shape
    return pl.pallas_call(
        functools.partial(_merge_kernel, alpha=alpha),
        grid=(bsz, lp // ROW_TILE),
        in_specs=[_rows(ROW_TILE, d), _rows(ROW_TILE, d)] + [_rows(ROW_TILE, WIDTH)] * N_BRANCH
        + [_whole((d, N_BRANCH * d)), _whole((N_BRANCH, WIDTH, d)), _whole((d, d)), _whole((1, d)), _whole((1, d))],
        out_specs=[_rows(ROW_TILE, d), _rows(ROW_TILE, d)],
        out_shape=[jax.ShapeDtypeStruct(h32.shape, F32), jax.ShapeDtypeStruct(h32.shape, BF16)],
        compiler_params=_cparams("parallel", "parallel"),
        name="merge",
    )(h32, h16, *branches, w_gate.astype(BF16), w_branch.astype(BF16), w_out.astype(BF16),
      ln_g.reshape(1, d), ln_b.reshape(1, d))


def _ffn_kernel(h32_ref, h16_ref, wg_ref, wv_ref, cwg_ref, cwv_ref, cbg_ref, cbv_ref, wd_ref, g_ref, b_ref,
                o32_ref, o16_ref, gcarry_ref, vcarry_ref, acc_ref, *, alpha):
    @pl.when(pl.program_id(1) == 0)
    def _():
        gcarry_ref[...] = jnp.zeros_like(gcarry_ref)
        vcarry_ref[...] = jnp.zeros_like(vcarry_ref)

    h16 = h16_ref[...]
    acc_ref[...] = jnp.zeros_like(acc_ref)

    def conv(x, carry_ref, cw_ref, cb_ref, j):
        carry = carry_ref[j]
        cw = cw_ref[j]
        out = cw[FFN_CONV - 1:FFN_CONV, :] * x + cb_ref[j]
        for d in range(1, FFN_CONV):
            k = FFN_CONV - 1 - d
            out = out + cw[k:k + 1, :] * _shift_rows(x, d, carry)
        carry_ref[j] = x[ROW_TILE - SUBLANES:, :]
        return out

    def body(j, carry):
        gate = conv(_dot(h16, wg_ref[j]), gcarry_ref, cwg_ref, cbg_ref, j)
        val = conv(_dot(h16, wv_ref[j]), vcarry_ref, cwv_ref, cbv_ref, j)
        act = (gate * _sigmoid(gate) * val).astype(BF16)
        acc_ref[...] += _dot(act, wd_ref[j])
        return carry

    lax.fori_loop(0, D_FF // FFN_CHUNK, body, 0)
    y = _layer_norm(alpha * h32_ref[...] + acc_ref[...], g_ref[...], b_ref[...])
    o32_ref[...] = y
    o16_ref[...] = y.astype(BF16)


def _ffn(h32, h16, w_up, conv_w, conv_b, w_down, ln_g, ln_b, alpha):
    bsz, lp, d = h32.shape
    nj = D_FF // FFN_CHUNK
    col_chunks = lambda w: w.reshape(w.shape[0], nj, FFN_CHUNK).transpose(1, 0, 2)
    wg = col_chunks(w_up[:, :D_FF]).astype(BF16)
    wv = col_chunks(w_up[:, D_FF:]).astype(BF16)
    conv_pad = jnp.pad(conv_w, ((0, SUBLANES - FFN_CONV), (0, 0)))
    cwg, cwv = col_chunks(conv_pad[:, :D_FF]), col_chunks(conv_pad[:, D_FF:])
    cbg, cbv = col_chunks(conv_b[None, :D_FF]), col_chunks(conv_b[None, D_FF:])
    wd = w_down.reshape(nj, FFN_CHUNK, d).astype(BF16)
    params = [wg, wv, cwg, cwv, cbg, cbv, wd, ln_g.reshape(1, d), ln_b.reshape(1, d)]
    return pl.pallas_call(
        functools.partial(_ffn_kernel, alpha=alpha),
        grid=(bsz, lp // ROW_TILE),
        in_specs=[_rows(ROW_TILE, d), _rows(ROW_TILE, d)] + [_whole(t.shape) for t in params],
        out_specs=[_rows(ROW_TILE, d), _rows(ROW_TILE, d)],
        out_shape=[jax.ShapeDtypeStruct(h32.shape, F32), jax.ShapeDtypeStruct(h32.shape, BF16)],
        scratch_shapes=[pltpu.VMEM((nj, SUBLANES, FFN_CHUNK), F32), pltpu.VMEM((nj, SUBLANES, FFN_CHUNK), F32),
                        pltpu.VMEM((ROW_TILE, d), F32)],
        compiler_params=_cparams("parallel", "arbitrary"),
        name="ffn",
    )(h32, h16, *params)


def kernel(x, meta_tokens, ln_in_g, ln_in_b, w_in, w_branch, w_out, ln_mix_g, ln_mix_b, lru_conv_w, lru_conv_b, lru_w_rg, lru_b_rg, lru_w_ig, lru_b_ig, lru_lambda, fox_b_f, mla_q_norm_g, mla_kv_norm_g, mla_w_uq, mla_w_ukv, rwkv_mu, rwkv_w0, rwkv_w2, rwkv_a0, rwkv_a2, rwkv_g2, rwkv_k_k, rwkv_k_a, rwkv_r_k, rwkv_gn_g, rwkv_gn_b, ffn_w_up, ffn_conv_w, ffn_conv_b, ffn_w_down, ln_ffn_g, ln_ffn_b):
    bsz, seq, d = x.shape
    depth = w_in.shape[0]
    alpha = (2.0 * depth) ** 0.25
    t_real = N_META + seq
    lp = -(-t_real // ROW_TILE) * ROW_TILE
    meta = jnp.broadcast_to(meta_tokens.astype(x.dtype)[None], (bsz, N_META, d))
    tokens = jnp.concatenate([meta, x, jnp.zeros((bsz, lp - t_real, d), x.dtype)], axis=1)
    h32, h16 = _ln_in(tokens, ln_in_g, ln_in_b)
    cos_t, sin_t = _rotary_tables(lp)

    widths = (LRU_WIDTH, LRU_WIDTH, WIDTH, WIDTH, WIDTH, HEADS, MLA_Q_RANK, MLA_KV_RANK, MLA_ROPE,
              RWKV_IN, N_BRANCH * D_MODEL)
    offs = np.concatenate([[0], np.cumsum(widths)])
    col = lambda w, i, j=None: w[:, int(offs[i]):int(offs[(i if j is None else j) + 1])]

    for l in range(depth):
        wl = w_in[l]
        o_a = _lru_branch(h16, col(wl, 0, 1), lru_conv_w[l], lru_conv_b[l], lru_w_rg[l], lru_b_rg[l],
                          lru_w_ig[l], lru_b_ig[l], lru_lambda[l])
        w_fox = jnp.concatenate([col(wl, 2, 4), _pad_cols(col(wl, 5), LANES)], axis=1)
        fq, fk, fv, f_cum = _fox_proj(h16, w_fox, fox_b_f[l])
        f_rows = jnp.swapaxes(f_cum[:, :, :SUBLANES], 1, 2)
        o_b = _attention(fq, fk, fv, 1.0 / math.sqrt(HEAD_DIM), f_rows=f_rows)
        mq, mqr, mk, mv = _mla_proj(h16, col(wl, 6), col(wl, 7), col(wl, 8), mla_q_norm_g[l], mla_kv_norm_g[l],
                                    mla_w_uq[l], mla_w_ukv[l], cos_t, sin_t)
        o_c = _attention(mq, mk, mv, 1.0 / math.sqrt(MLA_NOPE + MLA_ROPE), q_rope=mqr)
        o_d = _rwkv_branch(h16, col(wl, 9), rwkv_mu[l], rwkv_w0[l], rwkv_w2[l], rwkv_a0[l], rwkv_a2[l],
                           rwkv_g2[l], rwkv_k_k[l], rwkv_k_a[l], rwkv_r_k[l], rwkv_gn_g[l], rwkv_gn_b[l])
        h32, h16 = _merge(h32, h16, (o_a, o_b, o_c, o_d), col(wl, 10), w_branch[l], w_out[l],
                          ln_mix_g[l], ln_mix_b[l], alpha)
        h32, h16 = _ffn(h32, h16, ffn_w_up[l], ffn_conv_w[l], ffn_conv_b[l], ffn_w_down[l],
                        ln_ffn_g[l], ln_ffn_b[l], alpha)
    return h32[:, N_META:t_real]
```

```python
import functools
import math

import jax
import jax.numpy as jnp
import numpy as np
from jax import lax
from jax.experimental import pallas as pl
from jax.experimental.pallas import tpu as pltpu

F32 = jnp.float32
BF16 = jnp.bfloat16

D_MODEL = 1024
N_META = 16
LN_EPS = 1e-5
RMS_EPS = 1e-6
NEG_BIG = -1e30
LOG2E = 1.4426950408889634

LRU_WIDTH = 256
LRU_BLOCKS = 4
LRU_CONV = 4
LRU_C = 8.0
HEADS = 4
HEAD_DIM = 64
WIDTH = HEADS * HEAD_DIM
MLA_Q_RANK = 256
MLA_KV_RANK = 128
MLA_NOPE = 64
MLA_ROPE = 32
ROPE_BASE = 10000.0
RWKV_LORA = 128
RWKV_IN = 3 * WIDTH + RWKV_LORA
RWKV_GN_EPS = 64e-5
RWKV_DECAY_SCALE = math.exp(-0.5)
N_BRANCH = 4
D_FF = 2816
FFN_CONV = 3

SUBLANES = 8
LANES = 128
MXU_DIM = 256
ROW_TILE = 256
RWKV_CHUNK = 64
RWKV_SUB = 16
FFN_CHUNK = 256
ATTN_KEYS = 1024
ATTN_UNROLL = 1
VMEM_LIMIT = 56 * 1024 * 1024


def _cparams(*sem):
    return pltpu.CompilerParams(dimension_semantics=sem, vmem_limit_bytes=VMEM_LIMIT)


def _whole(shape):
    nd = len(shape)
    return pl.BlockSpec(shape, lambda *_: (0,) * nd)


def _rows(tile, width):
    return pl.BlockSpec((None, tile, width), lambda b, t: (b, t, 0))


def _iota(shape, dim):
    return lax.broadcasted_iota(jnp.int32, shape, dim)


def _dot(a, b):
    return jnp.dot(a, b, preferred_element_type=F32)


def _dot_nt(a, b):
    return lax.dot_general(a, b, (((1,), (1,)), ((), ())), preferred_element_type=F32)


def _dot_tn(a, b):
    return lax.dot_general(a, b, (((0,), (0,)), ((), ())), preferred_element_type=F32)


def _split(x, terms):
    out = []
    for _ in range(terms - 1):
        hi = x.astype(BF16)
        out.append(hi)
        x = x - hi.astype(F32)
    out.append(x.astype(BF16))
    return out


def _mdot(dot, a_terms, b_terms):
    order = max(len(a_terms), len(b_terms))
    acc = None
    for i, a in enumerate(a_terms):
        for j, b in enumerate(b_terms):
            if i + j < order:
                p = dot(a, b)
                acc = p if acc is None else acc + p
    return acc


def _sigmoid(x):
    return 1.0 / (1.0 + jnp.exp(-x))


def _softplus(x):
    return jnp.maximum(x, 0.0) + jnp.log(1.0 + jnp.exp(-jnp.abs(x)))


def _gelu_tanh(x):
    return 0.5 * x * (1.0 + jnp.tanh(math.sqrt(2.0 / math.pi) * (x + 0.044715 * (x * x * x))))


def _layer_norm(x, g, b):
    mu = jnp.mean(x, -1, keepdims=True)
    xc = x - mu
    var = jnp.mean(xc * xc, -1, keepdims=True)
    return xc * lax.rsqrt(var + LN_EPS) * g + b


def _shift_rows(x, d, carry):
    rows = _iota(x.shape, 0)
    out = pltpu.roll(x, d, 0)
    for r in range(d):
        src = SUBLANES - d + r
        out = jnp.where(rows == r, carry[src:src + 1, :], out)
    return out


def _cumsum_rows(x):
    rows = _iota(x.shape, 0)
    d = 1
    while d < x.shape[0]:
        x = x + jnp.where(rows >= d, pltpu.roll(x, d, 0), 0.0)
        d *= 2
    return x


def _linear_scan_rows(a, u):
    rows = _iota(a.shape, 0)
    d = 1
    while d < a.shape[0]:
        keep = rows >= d
        u = u + a * jnp.where(keep, pltpu.roll(u, d, 0), 0.0)
        a = a * jnp.where(keep, pltpu.roll(a, d, 0), 1.0)
        d *= 2
    return a, u


def _head_mask(shape, lane_dim, head):
    lane = _iota(shape, lane_dim)
    return (lane >= head * HEAD_DIM) & (lane < (head + 1) * HEAD_DIM)


def _stack_heads(x):
    return jnp.concatenate(
        [jnp.where(_head_mask(x.shape, 1, h), x, jnp.zeros_like(x)) for h in range(HEADS)], axis=0)


def _unstack_heads(x):
    r = x.shape[0] // HEADS
    out = x[0:r]
    for h in range(1, HEADS):
        out = out + x[h * r:(h + 1) * r]
    return out


def _head_sum(x, ones_bd):
    return _mdot(_dot, _split(x, 2), [ones_bd])


def _ln_in_kernel(x_ref, g_ref, b_ref, o32_ref, o16_ref):
    y = _layer_norm(x_ref[...], g_ref[...], b_ref[...])
    o32_ref[...] = y
    o16_ref[...] = y.astype(BF16)


def _ln_in(x, g, b):
    bsz, lp, d = x.shape
    return pl.pallas_call(
        _ln_in_kernel,
        grid=(bsz, lp // ROW_TILE),
        in_specs=[_rows(ROW_TILE, d), _whole((1, d)), _whole((1, d))],
        out_specs=[_rows(ROW_TILE, d), _rows(ROW_TILE, d)],
        out_shape=[jax.ShapeDtypeStruct(x.shape, F32), jax.ShapeDtypeStruct(x.shape, BF16)],
        compiler_params=_cparams("parallel", "parallel"),
        name="ln_in",
    )(x, g.reshape(1, d), b.reshape(1, d))


def _lru_kernel(h_ref, w_ref, cw_ref, cb_ref, wrg_ref, brg_ref, wig_ref, big_ref, lam_ref,
                o_ref, xcarry_ref, hcarry_ref):
    @pl.when(pl.program_id(1) == 0)
    def _():
        xcarry_ref[...] = jnp.zeros_like(xcarry_ref)
        hcarry_ref[...] = jnp.zeros_like(hcarry_ref)

    yx = _dot(h_ref[...], w_ref[...])
    y = yx[:, :LRU_WIDTH]
    x = yx[:, LRU_WIDTH:]
    xcarry = xcarry_ref[...]
    xc = cw_ref[LRU_CONV - 1:LRU_CONV, :] * x + cb_ref[...]
    for d in range(1, LRU_CONV):
        k = LRU_CONV - 1 - d
        xc = xc + cw_ref[k:k + 1, :] * _shift_rows(x, d, xcarry)
    xcarry_ref[...] = x[ROW_TILE - SUBLANES:, :]

    xc16 = xc.astype(BF16)
    r = _sigmoid(_dot(xc16, wrg_ref[...]) + brg_ref[...])
    i = _sigmoid(_dot(xc16, wig_ref[...]) + big_ref[...])
    log_a = (-LRU_C) * r * _softplus(-lam_ref[...])
    a = jnp.exp(log_a)
    th = jnp.tanh(log_a)
    u = jnp.sqrt(-2.0 * th / (1.0 - th)) * (i * xc)
    a_run, h = _linear_scan_rows(a, u)
    h = h + a_run * hcarry_ref[SUBLANES - 1:SUBLANES, :]
    hcarry_ref[...] = h[ROW_TILE - SUBLANES:, :]
    o_ref[...] = (_gelu_tanh(y) * h).astype(BF16)


def _block_diag(w):
    n, d, e = w.shape
    eye = jnp.eye(n, dtype=w.dtype)
    return (eye[:, None, :, None] * w[:, :, None, :]).reshape(n * d, n * e)


def _lru_branch(h16, w_yx, conv_w, conv_b, w_rg, b_rg, w_ig, b_ig, lam):
    bsz, lp, d = h16.shape
    row = lambda v: v.reshape(1, LRU_WIDTH)
    return pl.pallas_call(
        _lru_kernel,
        grid=(bsz, lp // ROW_TILE),
        in_specs=[_rows(ROW_TILE, d), _whole((d, 2 * LRU_WIDTH)),
                  _whole((LRU_CONV, LRU_WIDTH)), _whole((1, LRU_WIDTH)),
                  _whole((LRU_WIDTH, LRU_WIDTH)), _whole((1, LRU_WIDTH)),
                  _whole((LRU_WIDTH, LRU_WIDTH)), _whole((1, LRU_WIDTH)),
                  _whole((1, LRU_WIDTH))],
        out_specs=_rows(ROW_TILE, LRU_WIDTH),
        out_shape=jax.ShapeDtypeStruct((bsz, lp, LRU_WIDTH), BF16),
        scratch_shapes=[pltpu.VMEM((SUBLANES, LRU_WIDTH), F32), pltpu.VMEM((SUBLANES, LRU_WIDTH), F32)],
        compiler_params=_cparams("parallel", "arbitrary"),
        name="lru",
    )(h16, w_yx.astype(BF16), conv_w, row(conv_b), _block_diag(w_rg).astype(BF16), row(b_rg),
      _block_diag(w_ig).astype(BF16), row(b_ig), row(lam))


def _fox_proj_kernel(h_ref, w_ref, bf_ref, q_ref, k_ref, v_ref, f_ref, fcarry_ref):
    @pl.when(pl.program_id(1) == 0)
    def _():
        fcarry_ref[...] = jnp.zeros_like(fcarry_ref)

    z = _dot(h_ref[...], w_ref[...])
    q_ref[...] = z[:, 0:WIDTH].astype(BF16)
    k_ref[...] = z[:, WIDTH:2 * WIDTH].astype(BF16)
    v_ref[...] = z[:, 2 * WIDTH:3 * WIDTH].astype(BF16)
    log_f = -_softplus(-(z[:, 3 * WIDTH:] + bf_ref[...]))
    f = _cumsum_rows(log_f) + fcarry_ref[SUBLANES - 1:SUBLANES, :]
    fcarry_ref[...] = f[ROW_TILE - SUBLANES:, :]
    f_ref[...] = f


def _fox_proj(h16, w_qkvf, b_f):
    bsz, lp, d = h16.shape
    bf = jnp.zeros((1, LANES), F32).at[0, :HEADS].set(b_f)
    act = jax.ShapeDtypeStruct((bsz, lp, WIDTH), BF16)
    return pl.pallas_call(
        _fox_proj_kernel,
        grid=(bsz, lp // ROW_TILE),
        in_specs=[_rows(ROW_TILE, d), _whole((d, 3 * WIDTH + LANES)), _whole((1, LANES))],
        out_specs=[_rows(ROW_TILE, WIDTH)] * 3 + [_rows(ROW_TILE, LANES)],
        out_shape=[act, act, act, jax.ShapeDtypeStruct((bsz, lp, LANES), F32)],
        scratch_shapes=[pltpu.VMEM((SUBLANES, LANES), F32)],
        compiler_params=_cparams("parallel", "arbitrary"),
        name="fox_proj",
    )(h16, w_qkvf.astype(BF16), bf)


def _rms_norm(x, g):
    return x * lax.rsqrt(jnp.mean(x * x, -1, keepdims=True) + RMS_EPS) * g


def _mla_proj_kernel(h_ref, win_ref, qg_ref, kvg_ref, wq_ref, wkv_ref, cos_ref, sin_ref,
                     qn_ref, qr_ref, k01_ref, k23_ref, v_ref):
    z = _dot(h_ref[...], win_ref[...])
    c_q = z[:, 0:MLA_Q_RANK]
    c_kv = z[:, MLA_Q_RANK:MLA_Q_RANK + MLA_KV_RANK]
    k_r = z[:, MLA_Q_RANK + MLA_KV_RANK:MLA_Q_RANK + MLA_KV_RANK + LANES]
    k_r_swapped = z[:, MLA_Q_RANK + MLA_KV_RANK + LANES:]
    cos = cos_ref[...]
    sin = sin_ref[...]

    q = _dot(_rms_norm(c_q, qg_ref[...]).astype(BF16), wq_ref[...])
    qn_ref[...] = q[:, 0:WIDTH].astype(BF16)
    rope_w = HEADS * LANES
    for h in range(HEADS):
        plain = q[:, WIDTH + h * LANES:WIDTH + (h + 1) * LANES]
        swapped = q[:, WIDTH + rope_w + h * LANES:WIDTH + rope_w + (h + 1) * LANES]
        qr_ref[:, h * LANES:(h + 1) * LANES] = (plain * cos + swapped * sin).astype(BF16)

    kv = _dot(_rms_norm(c_kv, kvg_ref[...]).astype(BF16), wkv_ref[...])
    k_rope = (k_r * cos + k_r_swapped * sin).astype(BF16)
    for g, k_ref in enumerate((k01_ref, k23_ref)):
        k_ref[:, 0:LANES] = kv[:, g * LANES:(g + 1) * LANES].astype(BF16)
        k_ref[:, LANES:] = k_rope
    v_ref[...] = kv[:, WIDTH:].astype(BF16)


def _swap_halves(w):
    half = w.shape[-1] // 2
    return jnp.concatenate([w[..., half:], w[..., :half]], axis=-1)


def _pad_cols(w, width):
    return jnp.pad(w, ((0, 0), (0, width - w.shape[1])))


def _mla_proj(h16, w_cq, w_ckv, w_kr, q_norm_g, kv_norm_g, w_uq, w_ukv, cos_t, sin_t):
    bsz, lp, d = h16.shape
    w_in = jnp.concatenate([w_cq, w_ckv, _pad_cols(w_kr, LANES), _pad_cols(_swap_halves(w_kr), LANES)], axis=1)
    uq = w_uq.reshape(MLA_Q_RANK, HEADS, MLA_NOPE + MLA_ROPE)
    q_nope = uq[:, :, :MLA_NOPE].reshape(MLA_Q_RANK, WIDTH)
    q_rope = uq[:, :, MLA_NOPE:]
    lane_pad = lambda r: jnp.pad(r, ((0, 0), (0, 0), (0, LANES - MLA_ROPE))).reshape(MLA_Q_RANK, HEADS * LANES)
    wq = jnp.concatenate([q_nope, lane_pad(q_rope), lane_pad(_swap_halves(q_rope))], axis=1)
    ukv = w_ukv.reshape(MLA_KV_RANK, HEADS, MLA_NOPE + HEAD_DIM)
    wkv = jnp.concatenate([ukv[:, :, :MLA_NOPE].reshape(MLA_KV_RANK, WIDTH),
                           ukv[:, :, MLA_NOPE:].reshape(MLA_KV_RANK, WIDTH)], axis=1)
    table = pl.BlockSpec((ROW_TILE, LANES), lambda b, t: (t, 0))
    act = lambda w: jax.ShapeDtypeStruct((bsz, lp, w), BF16)
    return pl.pallas_call(
        _mla_proj_kernel,
        grid=(bsz, lp // ROW_TILE),
        in_specs=[_rows(ROW_TILE, d), _whole(w_in.shape), _whole((1, MLA_Q_RANK)), _whole((1, MLA_KV_RANK)),
                  _whole(wq.shape), _whole(wkv.shape), table, table],
        out_specs=[_rows(ROW_TILE, WIDTH), _rows(ROW_TILE, HEADS * LANES),
                   _rows(ROW_TILE, WIDTH), _rows(ROW_TILE, WIDTH), _rows(ROW_TILE, WIDTH)],
        out_shape=[act(WIDTH), act(HEADS * LANES), act(WIDTH), act(WIDTH), act(WIDTH)],
        compiler_params=_cparams("parallel", "parallel"),
        name="mla_proj",
    )(h16, w_in.astype(BF16), q_norm_g.reshape(1, -1), kv_norm_g.reshape(1, -1),
      wq.astype(BF16), wkv.astype(BF16), cos_t, sin_t)


def _rotary_tables(lp):
    inv = 1.0 / (ROPE_BASE ** (jnp.arange(0, MLA_ROPE, 2, dtype=F32) / MLA_ROPE))
    ang = jnp.arange(lp, dtype=F32)[:, None] * inv[None, :]
    cos, sin = jnp.cos(ang), jnp.sin(ang)
    pad = jnp.zeros((lp, LANES - MLA_ROPE), F32)
    return (jnp.concatenate([cos, cos, pad], axis=1), jnp.concatenate([-sin, sin, pad], axis=1))


def _attn_kernel(*refs, scale, has_rope, has_bias):
    it = iter(refs)
    q_ref = next(it)
    qr_ref = next(it) if has_rope else None
    k_refs = [next(it) for _ in range(2 if has_rope else 1)]
    v_ref = next(it)
    f_ref = next(it) if has_bias else None
    o_ref = next(it)
    m_ref, l_ref, acc_ref = next(it), next(it), next(it)

    tq = q_ref.shape[0]
    qi = pl.program_id(1)
    heads_per_group = HEADS // len(k_refs)
    q_stacks = []
    for g in range(len(k_refs)):
        parts = []
        for h in range(g * heads_per_group, (g + 1) * heads_per_group):
            if has_rope:
                qg = q_ref[:, g * LANES:(g + 1) * LANES]
                qh = jnp.where(_head_mask(qg.shape, 1, h % heads_per_group), qg, jnp.zeros_like(qg))
                qh = jnp.concatenate([qh, qr_ref[:, h * LANES:(h + 1) * LANES]], axis=1)
            else:
                q = q_ref[...]
                qh = jnp.where(_head_mask(q.shape, 1, h), q, jnp.zeros_like(q))
            parts.append(qh)
        q_stacks.append(jnp.concatenate(parts, axis=0))

    m_ref[...] = jnp.full(m_ref.shape, NEG_BIG, F32)
    l_ref[...] = jnp.zeros_like(l_ref)
    acc_ref[...] = jnp.zeros_like(acc_ref)
    c = scale * LOG2E
    if has_bias:
        f_q0 = f_ref[:, pl.ds(pl.multiple_of(qi * tq, tq), LANES)][:, 0:1]

    def block(start, tk, masked):
        vb = v_ref[pl.ds(start, tk), :]
        scores = [_dot_nt(qs, k_ref[pl.ds(start, tk), :]) for qs, k_ref in zip(q_stacks, k_refs)]
        if has_bias:
            bias = (f_q0 - f_ref[:, pl.ds(start, tk)]) * LOG2E
        if masked:
            keep = _iota((tq, tk), 0) >= _iota((tq, tk), 1)
        for h in range(HEADS):
            rows = slice(h * tq, (h + 1) * tq)
            in_group = h % heads_per_group
            sh = scores[h // heads_per_group][in_group * tq:(in_group + 1) * tq] * c
            if has_bias:
                sh = sh + bias[h:h + 1, :]
            if masked:
                sh = jnp.where(keep, sh, NEG_BIG)
            m_old = m_ref[rows]
            m_new = jnp.maximum(m_old, jnp.max(sh, axis=1, keepdims=True))
            p = jnp.exp2(sh - pltpu.repeat(m_new, tk // LANES, axis=1))
            alpha = jnp.exp2(m_old - m_new)
            l_ref[rows] = alpha * l_ref[rows] + jnp.sum(p, axis=1, keepdims=True)
            m_ref[rows] = m_new
            acc_ref[rows] = acc_ref[rows] * pltpu.repeat(alpha, WIDTH // LANES, axis=1) + _dot(p.astype(BF16), vb)

    per = ATTN_KEYS // tq
    step_keys = ATTN_UNROLL * ATTN_KEYS
    n_steps = (qi * tq) // step_keys

    def body(j, carry):
        base = pl.multiple_of(j * step_keys, step_keys)
        for u in range(ATTN_UNROLL):
            block(base + u * ATTN_KEYS, ATTN_KEYS, False)
        return carry

    lax.fori_loop(0, n_steps, body, 0)
    left = qi - n_steps * (step_keys // tq)
    rest = pl.multiple_of(n_steps * step_keys, step_keys)
    for u in range(ATTN_UNROLL - 1):
        @pl.when(left >= (u + 1) * per)
        def _():
            block(rest + u * ATTN_KEYS, ATTN_KEYS, False)
    tail = left % per
    for r in range(per - 1):
        @pl.when(tail > r)
        def _():
            block(pl.multiple_of((qi - tail + r) * tq, tq), tq, False)

    block(pl.multiple_of(qi * tq, tq), tq, True)

    out = jnp.zeros((tq, WIDTH), F32)
    for h in range(HEADS):
        rows = slice(h * tq, (h + 1) * tq)
        oh = acc_ref[rows] / pltpu.repeat(l_ref[rows], WIDTH // LANES, axis=1)
        out = jnp.where(_head_mask(out.shape, 1, h), oh, out)
    o_ref[...] = out.astype(BF16)


def _attention(q, keys, v, scale, q_rope=None, f_rows=None):
    bsz, lp, _ = q.shape
    tq = ROW_TILE
    resident = lambda w: pl.BlockSpec((None, lp, w), lambda b, t: (b, 0, 0))
    in_specs = [_rows(tq, WIDTH)]
    args = [q]
    if q_rope is not None:
        in_specs.append(_rows(tq, HEADS * LANES))
        args.append(q_rope)
    in_specs += [resident(WIDTH)] * (len(keys) + 1)
    args += [*keys, v]
    if f_rows is not None:
        in_specs.append(pl.BlockSpec((None, SUBLANES, lp), lambda b, t: (b, 0, 0)))
        args.append(f_rows)
    kern = functools.partial(_attn_kernel, scale=scale, has_rope=q_rope is not None,
                             has_bias=f_rows is not None)
    return pl.pallas_call(
        kern,
        grid=(bsz, lp // tq),
        in_specs=in_specs,
        out_specs=_rows(tq, WIDTH),
        out_shape=jax.ShapeDtypeStruct((bsz, lp, WIDTH), BF16),
        scratch_shapes=[pltpu.VMEM((HEADS * tq, LANES), F32), pltpu.VMEM((HEADS * tq, LANES), F32),
                        pltpu.VMEM((HEADS * tq, WIDTH), F32)],
        compiler_params=_cparams("parallel", "arbitrary"),
        name="attn_bias" if f_rows is not None else "attn_rope",
    )(*args)


def _unit_lower_inverse(m, sub_blk):
    n = m.shape[0]
    eye = (_iota((n, n), 0) == _iota((n, n), 1)).astype(F32)
    md = jnp.where(sub_blk, m, 0.0)
    mo = m - md
    mm = lambda a, b: _dot(a.astype(BF16), b.astype(BF16))
    x = eye + md
    p = md
    for _ in range(int(math.log2(RWKV_SUB)) - 1):
        p = mm(p, p)
        x = mm(x, eye + p)
    pm = mm(x, mo)
    pm2 = mm(pm, pm)
    series = eye + pm + pm2 + mm(pm, pm2)
    return mm(series, x)


def _rwkv_kernel(h_ref, w_ref, mu_ref, w0_ref, w2_ref, a0_ref, a2_ref, g2_ref, kk_ref, ka_ref,
                 rk_ref, gng_ref, gnb_ref, ones_ref, o_ref, pcarry_ref, state_ref, y_ref):
    @pl.when(pl.program_id(1) == 0)
    def _():
        pcarry_ref[...] = jnp.zeros_like(pcarry_ref)
        state_ref[...] = jnp.zeros_like(state_ref)

    p_in = _dot(h_ref[...], w_ref[...])
    prev = _shift_rows(p_in, 1, pcarry_ref[...])
    pcarry_ref[...] = p_in[ROW_TILE - SUBLANES:, :]
    p = p_in + (prev - p_in) * mu_ref[...]
    r = p[:, 0:WIDTH]
    k = p[:, WIDTH:2 * WIDTH]
    v = p[:, 2 * WIDTH:3 * WIDTH]
    lora = p[:, 3 * WIDTH:]
    ones_bd = ones_ref[...]

    z = w0_ref[...] + _dot(jnp.tanh(lora).astype(BF16), w2_ref[...])
    log_w = (-RWKV_DECAY_SCALE) * _sigmoid(z)
    a_gate = _sigmoid(a0_ref[...] + _dot(lora.astype(BF16), a2_ref[...]))
    gate = _dot(_sigmoid(lora).astype(BF16), g2_ref[...])
    kk = k * kk_ref[...]
    kk = kk * lax.rsqrt(jnp.maximum(_head_sum(kk * kk, ones_bd), 1e-24))
    kf = k * (1.0 + (a_gate - 1.0) * ka_ref[...])
    a_vec = -kk
    b_vec = kk * a_gate

    n = HEADS * RWKV_CHUNK
    ri, ci = _iota((n, n), 0), _iota((n, n), 1)
    strict = ri > ci
    incl = ri >= ci
    sub_blk = (ri // RWKV_SUB) == (ci // RWKV_SUB)
    diag = ri == ci
    mid = RWKV_CHUNK // 2 - 1
    sp = lambda t: _split(t, 1)
    sp_state = lambda t: _split(t, 2)
    mm = lambda a, b: _mdot(_dot, a, b)
    mm_nt = lambda a, b: _mdot(_dot_nt, a, b)
    mm_tn = lambda a, b: _mdot(_dot_tn, a, b)

    for c in range(ROW_TILE // RWKV_CHUNK):
        rows = slice(c * RWKV_CHUNK, (c + 1) * RWKV_CHUNK)
        lw = log_w[rows]
        g_in = _cumsum_rows(lw)
        g_ex = g_in - lw
        g_mid = g_in[mid:mid + 1, :]
        g_end = g_in[RWKV_CHUNK - 1:RWKV_CHUNK, :]
        decay_out = jnp.exp(g_mid - g_in)
        a_c, b_c, k_c, r_c, v_c = a_vec[rows], b_vec[rows], kf[rows], r[rows], v[rows]
        a_t = sp(_stack_heads(a_c * jnp.exp(g_ex - g_mid)))
        r_t = sp(_stack_heads(r_c * jnp.exp(g_in - g_mid)))
        b_t = sp(_stack_heads(b_c * decay_out))
        k_t = sp(_stack_heads(k_c * decay_out))
        a_0 = sp(_stack_heads(a_c * jnp.exp(g_ex)))
        r_0 = _stack_heads(r_c * jnp.exp(g_in))
        to_end = jnp.exp(g_end - g_in)
        b_e = sp(_stack_heads(b_c * to_end))
        k_e = sp(_stack_heads(k_c * to_end))
        v_s = sp(_stack_heads(v_c))

        m_ab = jnp.where(strict, mm_nt(a_t, b_t), 0.0)
        m_ak = sp(jnp.where(strict, mm_nt(a_t, k_t), 0.0))
        m_rb = sp(jnp.where(incl, mm_nt(r_t, b_t), 0.0))
        m_rk = sp(jnp.where(incl, mm_nt(r_t, k_t), 0.0))
        t_inv = sp(_unit_lower_inverse(m_ab, sub_blk))

        w_a = sp(mm(t_inv, a_0))
        u_v = sp(mm(t_inv, sp(mm(m_ak, v_s))))
        r_eff = sp_state(r_0 + mm(m_rb, w_a))
        y_v = mm(m_rb, u_v) + mm(m_rk, v_s)
        g_mat = jnp.where(diag, jnp.exp(g_end), 0.0) + mm_tn(b_e, w_a)
        d_mat = mm_tn(b_e, u_v) + mm_tn(k_e, v_s)

        state = sp_state(state_ref[...])
        y_ref[rows, :] = _unstack_heads(mm(r_eff, state) + y_v)
        state_ref[...] = mm(sp_state(g_mat), state) + d_mat

    y = y_ref[...]
    inv_n = 1.0 / HEAD_DIM
    y_mu = _head_sum(y, ones_bd) * inv_n
    yc = y - y_mu
    y_var = _head_sum(yc * yc, ones_bd) * inv_n
    y = yc * lax.rsqrt(y_var + RWKV_GN_EPS) * gng_ref[...] + gnb_ref[...]
    y = y + _head_sum(r * kf * rk_ref[...], ones_bd) * v
    o_ref[...] = (y * gate).astype(BF16)


def _rwkv_branch(h16, w_p, mu, w0, w2, a0, a2, g2, k_k, k_a, r_k, gn_g, gn_b):
    bsz, lp, d = h16.shape
    row = lambda t: t.reshape(1, -1)
    lora_rows = lambda w, lo: jnp.zeros((RWKV_LORA, WIDTH), F32).at[lo:lo + w.shape[0]].set(w).astype(BF16)
    ones_bd = _block_diag(jnp.ones((HEADS, HEAD_DIM, HEAD_DIM), F32)).astype(BF16)
    params = [w_p.astype(BF16), row(mu), row(w0), lora_rows(w2, 0), row(a0), lora_rows(a2, 32),
              lora_rows(g2, 64), row(k_k), row(k_a), row(r_k), row(gn_g), row(gn_b), ones_bd]
    return pl.pallas_call(
        _rwkv_kernel,
        grid=(bsz, lp // ROW_TILE),
        in_specs=[_rows(ROW_TILE, d)] + [_whole(t.shape) for t in params],
        out_specs=_rows(ROW_TILE, WIDTH),
        out_shape=jax.ShapeDtypeStruct((bsz, lp, WIDTH), BF16),
        scratch_shapes=[pltpu.VMEM((SUBLANES, RWKV_IN), F32), pltpu.VMEM((WIDTH, WIDTH), F32),
                        pltpu.VMEM((ROW_TILE, WIDTH), F32)],
        compiler_params=_cparams("parallel", "arbitrary"),
        name="rwkv",
    )(h16, *params)


def _merge_kernel(h32_ref, h16_ref, oa_ref, ob_ref, oc_ref, od_ref, wg_ref, wb_ref, wo_ref, g_ref, b_ref,
                  o32_ref, o16_ref, *, alpha):
    h16 = h16_ref[...]
    acc = None
    for n, o_ref in enumerate((oa_ref, ob_ref, oc_ref, od_ref)):
        gate = _sigmoid(_dot(h16, wg_ref[:, n * D_MODEL:(n + 1) * D_MODEL]))
        term = gate * _dot(o_ref[...], wb_ref[n])
        acc = term if acc is None else acc + term
    mixed = _dot(acc.astype(BF16), wo_ref[...])
    y = _layer_norm(alpha * h32_ref[...] + mixed, g_ref[...], b_ref[...])
    o32_ref[...] = y
    o16_ref[...] = y.astype(BF16)


def _merge(h32, h16, branches, w_gate, w_branch, w_out, ln_g, ln_b, alpha):
    bsz, lp, d = h32.shape
    return pl.pallas_call(
        functools.partial(_merge_kernel, alpha=alpha),
        grid=(bsz, lp // ROW_TILE),
        in_specs=[_rows(ROW_TILE, d), _rows(ROW_TILE, d)] + [_rows(ROW_TILE, WIDTH)] * N_BRANCH
        + [_whole((d, N_BRANCH * d)), _whole((N_BRANCH, WIDTH, d)), _whole((d, d)), _whole((1, d)), _whole((1, d))],
        out_specs=[_rows(ROW_TILE, d), _rows(ROW_TILE, d)],
        out_shape=[jax.ShapeDtypeStruct(h32.shape, F32), jax.ShapeDtypeStruct(h32.shape, BF16)],
        compiler_params=_cparams("parallel", "parallel"),
        name="merge",
    )(h32, h16, *branches, w_gate.astype(BF16), w_branch.astype(BF16), w_out.astype(BF16),
      ln_g.reshape(1, d), ln_b.reshape(1, d))


def _ffn_kernel(h32_ref, h16_ref, wu_ref, cw_ref, cb_ref, wd_ref, g_ref, b_ref,
                o32_ref, o16_ref, carry_ref, acc_ref, *, alpha):
    @pl.when(pl.program_id(1) == 0)
    def _():
        carry_ref[...] = jnp.zeros_like(carry_ref)

    h16 = h16_ref[...]

    def conv_cols(lo):
        cols = slice(lo, lo + FFN_CHUNK)
        x = _dot(h16, wu_ref[:, cols])
        carry = carry_ref[:, cols]
        out = cw_ref[FFN_CONV - 1:FFN_CONV, cols] * x + cb_ref[:, cols]
        for d in range(1, FFN_CONV):
            k = FFN_CONV - 1 - d
            out = out + cw_ref[k:k + 1, cols] * _shift_rows(x, d, carry)
        carry_ref[:, cols] = x[ROW_TILE - SUBLANES:, :]
        return out

    for j in range(D_FF // FFN_CHUNK):
        gate = conv_cols(j * FFN_CHUNK)
        val = conv_cols(D_FF + j * FFN_CHUNK)
        act = (gate * _sigmoid(gate) * val).astype(BF16)
        part = _dot(act, wd_ref[j * FFN_CHUNK:(j + 1) * FFN_CHUNK, :])
        if j == 0:
            acc_ref[...] = part
        else:
            acc_ref[...] += part
    y = _layer_norm(alpha * h32_ref[...] + acc_ref[...], g_ref[...], b_ref[...])
    o32_ref[...] = y
    o16_ref[...] = y.astype(BF16)


def _ffn(h32, h16, w_up, conv_w, conv_b, w_down, ln_g, ln_b, alpha):
    bsz, lp, d = h32.shape
    params = [w_up.astype(BF16), jnp.pad(conv_w, ((0, SUBLANES - FFN_CONV), (0, 0))), conv_b.reshape(1, -1),
              w_down.astype(BF16), ln_g.reshape(1, d), ln_b.reshape(1, d)]
    return pl.pallas_call(
        functools.partial(_ffn_kernel, alpha=alpha),
        grid=(bsz, lp // ROW_TILE),
        in_specs=[_rows(ROW_TILE, d), _rows(ROW_TILE, d)] + [_whole(t.shape) for t in params],
        out_specs=[_rows(ROW_TILE, d), _rows(ROW_TILE, d)],
        out_shape=[jax.ShapeDtypeStruct(h32.shape, F32), jax.ShapeDtypeStruct(h32.shape, BF16)],
        scratch_shapes=[pltpu.VMEM((SUBLANES, 2 * D_FF), F32), pltpu.VMEM((ROW_TILE, d), F32)],
        compiler_params=_cparams("parallel", "arbitrary"),
        name="ffn",
    )(h32, h16, *params)


def kernel(x, meta_tokens, ln_in_g, ln_in_b, w_in, w_branch, w_out, ln_mix_g, ln_mix_b, lru_conv_w, lru_conv_b, lru_w_rg, lru_b_rg, lru_w_ig, lru_b_ig, lru_lambda, fox_b_f, mla_q_norm_g, mla_kv_norm_g, mla_w_uq, mla_w_ukv, rwkv_mu, rwkv_w0, rwkv_w2, rwkv_a0, rwkv_a2, rwkv_g2, rwkv_k_k, rwkv_k_a, rwkv_r_k, rwkv_gn_g, rwkv_gn_b, ffn_w_up, ffn_conv_w, ffn_conv_b, ffn_w_down, ln_ffn_g, ln_ffn_b):
    bsz, seq, d = x.shape
    depth = w_in.shape[0]
    alpha = (2.0 * depth) ** 0.25
    t_real = N_META + seq
    lp = -(-t_real // ROW_TILE) * ROW_TILE
    meta = jnp.broadcast_to(meta_tokens.astype(x.dtype)[None], (bsz, N_META, d))
    tokens = jnp.concatenate([meta, x, jnp.zeros((bsz, lp - t_real, d), x.dtype)], axis=1)
    h32, h16 = _ln_in(tokens, ln_in_g, ln_in_b)
    cos_t, sin_t = _rotary_tables(lp)

    widths = (LRU_WIDTH, LRU_WIDTH, WIDTH, WIDTH, WIDTH, HEADS, MLA_Q_RANK, MLA_KV_RANK, MLA_ROPE,
              RWKV_IN, N_BRANCH * D_MODEL)
    offs = np.concatenate([[0], np.cumsum(widths)])
    col = lambda w, i, j=None: w[:, int(offs[i]):int(offs[(i if j is None else j) + 1])]

    for l in range(depth):
        wl = w_in[l]
        o_a = _lru_branch(h16, col(wl, 0, 1), lru_conv_w[l], lru_conv_b[l], lru_w_rg[l], lru_b_rg[l],
                          lru_w_ig[l], lru_b_ig[l], lru_lambda[l])
        w_fox = jnp.concatenate([col(wl, 2, 4), _pad_cols(col(wl, 5), LANES)], axis=1)
        fq, fk, fv, f_cum = _fox_proj(h16, w_fox, fox_b_f[l])
        f_rows = jnp.swapaxes(f_cum[:, :, :SUBLANES], 1, 2)
        o_b = _attention(fq, (fk,), fv, 1.0 / math.sqrt(HEAD_DIM), f_rows=f_rows)
        mq, mqr, mk01, mk23, mv = _mla_proj(h16, col(wl, 6), col(wl, 7), col(wl, 8), mla_q_norm_g[l],
                                            mla_kv_norm_g[l], mla_w_uq[l], mla_w_ukv[l], cos_t, sin_t)
        o_c = _attention(mq, (mk01, mk23), mv, 1.0 / math.sqrt(MLA_NOPE + MLA_ROPE), q_rope=mqr)
        o_d = _rwkv_branch(h16, col(wl, 9), rwkv_mu[l], rwkv_w0[l], rwkv_w2[l], rwkv_a0[l], rwkv_a2[l],
                           rwkv_g2[l], rwkv_k_k[l], rwkv_k_a[l], rwkv_r_k[l], rwkv_gn_g[l], rwkv_gn_b[l])
        h32, h16 = _merge(h32, h16, (o_a, o_b, o_c, o_d), col(wl, 10), w_branch[l], w_out[l],
                          ln_mix_g[l], ln_mix_b[l], alpha)
        h32, h16 = _ffn(h32, h16, ffn_w_up[l], ffn_conv_w[l], ffn_conv_b[l], ffn_w_down[l],
                        ln_ffn_g[l], ln_ffn_b[l], alpha)
    return h32[:, N_META:t_real]
```

```python
import functools
import math

import jax
import jax.numpy as jnp
import numpy as np
from jax import lax
from jax.experimental import pallas as pl
from jax.experimental.pallas import tpu as pltpu

F32 = jnp.float32
BF16 = jnp.bfloat16

D_MODEL = 1024
N_META = 16
LN_EPS = 1e-5
RMS_EPS = 1e-6
NEG_BIG = -1e30
LOG2E = 1.4426950408889634

LRU_WIDTH = 256
LRU_BLOCKS = 4
LRU_CONV = 4
LRU_C = 8.0
HEADS = 4
HEAD_DIM = 64
WIDTH = HEADS * HEAD_DIM
MLA_Q_RANK = 256
MLA_KV_RANK = 128
MLA_NOPE = 64
MLA_ROPE = 32
ROPE_BASE = 10000.0
RWKV_LORA = 128
RWKV_IN = 3 * WIDTH + RWKV_LORA
RWKV_GN_EPS = 64e-5
RWKV_DECAY_SCALE = math.exp(-0.5)
N_BRANCH = 4
D_FF = 2816
FFN_CONV = 3

SUBLANES = 8
LANES = 128
MXU_DIM = 256
ROW_TILE = 256
RWKV_CHUNK = 64
RWKV_SUB = 16
FFN_CHUNK = 256
ATTN_KEYS = 1024
ATTN_UNROLL = 1
VMEM_LIMIT = 56 * 1024 * 1024


def _cparams(*sem):
    return pltpu.CompilerParams(dimension_semantics=sem, vmem_limit_bytes=VMEM_LIMIT)


def _whole(shape):
    nd = len(shape)
    return pl.BlockSpec(shape, lambda *_: (0,) * nd)


def _rows(tile, width):
    return pl.BlockSpec((None, tile, width), lambda b, t: (b, t, 0))


def _iota(shape, dim):
    return lax.broadcasted_iota(jnp.int32, shape, dim)


def _dot(a, b):
    return jnp.dot(a, b, preferred_element_type=F32)


def _dot_nt(a, b):
    return lax.dot_general(a, b, (((1,), (1,)), ((), ())), preferred_element_type=F32)


def _dot_tn(a, b):
    return lax.dot_general(a, b, (((0,), (0,)), ((), ())), preferred_element_type=F32)


def _split(x, terms):
    out = []
    for _ in range(terms - 1):
        hi = x.astype(BF16)
        out.append(hi)
        x = x - hi.astype(F32)
    out.append(x.astype(BF16))
    return out


def _mdot(dot, a_terms, b_terms):
    order = max(len(a_terms), len(b_terms))
    acc = None
    for i, a in enumerate(a_terms):
        for j, b in enumerate(b_terms):
            if i + j < order:
                p = dot(a, b)
                acc = p if acc is None else acc + p
    return acc


def _sigmoid(x):
    return 1.0 / (1.0 + jnp.exp(-x))


def _softplus(x):
    return jnp.maximum(x, 0.0) + jnp.log(1.0 + jnp.exp(-jnp.abs(x)))


def _gelu_tanh(x):
    return 0.5 * x * (1.0 + jnp.tanh(math.sqrt(2.0 / math.pi) * (x + 0.044715 * (x * x * x))))


def _layer_norm(x, g, b):
    mu = jnp.mean(x, -1, keepdims=True)
    xc = x - mu
    var = jnp.mean(xc * xc, -1, keepdims=True)
    return xc * lax.rsqrt(var + LN_EPS) * g + b


def _shift_rows(x, d, carry):
    rolled = pltpu.roll(x, d, 0)
    head = rolled[:SUBLANES]
    rows = _iota(head.shape, 0)
    for r in range(d):
        src = SUBLANES - d + r
        head = jnp.where(rows == r, carry[src:src + 1, :], head)
    return jnp.concatenate([head, rolled[SUBLANES:]], axis=0)


def _cumsum_rows(x):
    rows = _iota(x.shape, 0)
    d = 1
    while d < x.shape[0]:
        x = x + jnp.where(rows >= d, pltpu.roll(x, d, 0), 0.0)
        d *= 2
    return x


def _linear_scan_rows(a, u):
    rows = _iota(a.shape, 0)
    d = 1
    while d < a.shape[0]:
        keep = rows >= d
        u = u + a * jnp.where(keep, pltpu.roll(u, d, 0), 0.0)
        a = a * jnp.where(keep, pltpu.roll(a, d, 0), 1.0)
        d *= 2
    return a, u


def _tile_lanes(x, n):
    return jnp.concatenate([x] * n, axis=1)


def _head_mask(shape, lane_dim, head):
    lane = _iota(shape, lane_dim)
    return (lane >= head * HEAD_DIM) & (lane < (head + 1) * HEAD_DIM)


def _stack_heads(x):
    return jnp.concatenate(
        [jnp.where(_head_mask(x.shape, 1, h), x, jnp.zeros_like(x)) for h in range(HEADS)], axis=0)


def _unstack_heads(x):
    r = x.shape[0] // HEADS
    out = x[0:r]
    for h in range(1, HEADS):
        out = out + x[h * r:(h + 1) * r]
    return out


def _head_sum(x, ones_bd):
    return _mdot(_dot, _split(x, 2), [ones_bd])


def _ln_in_kernel(x_ref, g_ref, b_ref, o32_ref, o16_ref):
    y = _layer_norm(x_ref[...], g_ref[...], b_ref[...])
    o32_ref[...] = y
    o16_ref[...] = y.astype(BF16)


def _ln_in(x, g, b):
    bsz, lp, d = x.shape
    return pl.pallas_call(
        _ln_in_kernel,
        grid=(bsz, lp // ROW_TILE),
        in_specs=[_rows(ROW_TILE, d), _whole((1, d)), _whole((1, d))],
        out_specs=[_rows(ROW_TILE, d), _rows(ROW_TILE, d)],
        out_shape=[jax.ShapeDtypeStruct(x.shape, F32), jax.ShapeDtypeStruct(x.shape, BF16)],
        compiler_params=_cparams("parallel", "parallel"),
        name="ln_in",
    )(x, g.reshape(1, d), b.reshape(1, d))


def _lru_kernel(h_ref, w_ref, cw_ref, cb_ref, wrg_ref, brg_ref, wig_ref, big_ref, lam_ref,
                o_ref, xcarry_ref, hcarry_ref):
    @pl.when(pl.program_id(1) == 0)
    def _():
        xcarry_ref[...] = jnp.zeros_like(xcarry_ref)
        hcarry_ref[...] = jnp.zeros_like(hcarry_ref)

    yx = _dot(h_ref[...], w_ref[...])
    y = yx[:, :LRU_WIDTH]
    x = yx[:, LRU_WIDTH:]
    xcarry = xcarry_ref[...]
    xc = cw_ref[LRU_CONV - 1:LRU_CONV, :] * x + cb_ref[...]
    for d in range(1, LRU_CONV):
        k = LRU_CONV - 1 - d
        xc = xc + cw_ref[k:k + 1, :] * _shift_rows(x, d, xcarry)
    xcarry_ref[...] = x[ROW_TILE - SUBLANES:, :]

    xc16 = xc.astype(BF16)
    r = _sigmoid(_dot(xc16, wrg_ref[...]) + brg_ref[...])
    i = _sigmoid(_dot(xc16, wig_ref[...]) + big_ref[...])
    log_a = (-LRU_C) * r * _softplus(-lam_ref[...])
    a = jnp.exp(log_a)
    th = jnp.tanh(log_a)
    u = jnp.sqrt(-2.0 * th / (1.0 - th)) * (i * xc)
    a_run, h = _linear_scan_rows(a, u)
    h = h + a_run * hcarry_ref[SUBLANES - 1:SUBLANES, :]
    hcarry_ref[...] = h[ROW_TILE - SUBLANES:, :]
    o_ref[...] = (_gelu_tanh(y) * h).astype(BF16)


def _block_diag(w):
    n, d, e = w.shape
    eye = jnp.eye(n, dtype=w.dtype)
    return (eye[:, None, :, None] * w[:, :, None, :]).reshape(n * d, n * e)


def _lru_branch(h16, w_yx, conv_w, conv_b, w_rg, b_rg, w_ig, b_ig, lam):
    bsz, lp, d = h16.shape
    row = lambda v: v.reshape(1, LRU_WIDTH)
    return pl.pallas_call(
        _lru_kernel,
        grid=(bsz, lp // ROW_TILE),
        in_specs=[_rows(ROW_TILE, d), _whole((d, 2 * LRU_WIDTH)),
                  _whole((LRU_CONV, LRU_WIDTH)), _whole((1, LRU_WIDTH)),
                  _whole((LRU_WIDTH, LRU_WIDTH)), _whole((1, LRU_WIDTH)),
                  _whole((LRU_WIDTH, LRU_WIDTH)), _whole((1, LRU_WIDTH)),
                  _whole((1, LRU_WIDTH))],
        out_specs=_rows(ROW_TILE, LRU_WIDTH),
        out_shape=jax.ShapeDtypeStruct((bsz, lp, LRU_WIDTH), BF16),
        scratch_shapes=[pltpu.VMEM((SUBLANES, LRU_WIDTH), F32), pltpu.VMEM((SUBLANES, LRU_WIDTH), F32)],
        compiler_params=_cparams("parallel", "arbitrary"),
        name="lru",
    )(h16, w_yx.astype(BF16), conv_w, row(conv_b), _block_diag(w_rg).astype(BF16), row(b_rg),
      _block_diag(w_ig).astype(BF16), row(b_ig), row(lam))


def _fox_proj_kernel(h_ref, w_ref, bf_ref, q_ref, k_ref, v_ref, f_ref, fcarry_ref):
    @pl.when(pl.program_id(1) == 0)
    def _():
        fcarry_ref[...] = jnp.zeros_like(fcarry_ref)

    z = _dot(h_ref[...], w_ref[...])
    q_ref[...] = z[:, 0:WIDTH].astype(BF16)
    k_ref[...] = z[:, WIDTH:2 * WIDTH].astype(BF16)
    v_ref[...] = z[:, 2 * WIDTH:3 * WIDTH].astype(BF16)
    log_f = -_softplus(-(z[:, 3 * WIDTH:] + bf_ref[...]))
    f = _cumsum_rows(log_f) + fcarry_ref[SUBLANES - 1:SUBLANES, :]
    fcarry_ref[...] = f[ROW_TILE - SUBLANES:, :]
    f_ref[...] = f


def _fox_proj(h16, w_qkvf, b_f):
    bsz, lp, d = h16.shape
    bf = jnp.zeros((1, LANES), F32).at[0, :HEADS].set(b_f)
    act = jax.ShapeDtypeStruct((bsz, lp, WIDTH), BF16)
    return pl.pallas_call(
        _fox_proj_kernel,
        grid=(bsz, lp // ROW_TILE),
        in_specs=[_rows(ROW_TILE, d), _whole((d, 3 * WIDTH + LANES)), _whole((1, LANES))],
        out_specs=[_rows(ROW_TILE, WIDTH)] * 3 + [_rows(ROW_TILE, LANES)],
        out_shape=[act, act, act, jax.ShapeDtypeStruct((bsz, lp, LANES), F32)],
        scratch_shapes=[pltpu.VMEM((SUBLANES, LANES), F32)],
        compiler_params=_cparams("parallel", "arbitrary"),
        name="fox_proj",
    )(h16, w_qkvf.astype(BF16), bf)


def _rms_norm(x, g):
    return x * lax.rsqrt(jnp.mean(x * x, -1, keepdims=True) + RMS_EPS) * g


def _mla_proj_kernel(h_ref, win_ref, qg_ref, kvg_ref, wq_ref, wkv_ref, cos_ref, sin_ref,
                     qn_ref, qr_ref, k01_ref, k23_ref, v_ref):
    z = _dot(h_ref[...], win_ref[...])
    c_q = z[:, 0:MLA_Q_RANK]
    c_kv = z[:, MLA_Q_RANK:MLA_Q_RANK + MLA_KV_RANK]
    k_r = z[:, MLA_Q_RANK + MLA_KV_RANK:MLA_Q_RANK + MLA_KV_RANK + LANES]
    k_r_swapped = z[:, MLA_Q_RANK + MLA_KV_RANK + LANES:]
    cos = cos_ref[...]
    sin = sin_ref[...]

    q = _dot(_rms_norm(c_q, qg_ref[...]).astype(BF16), wq_ref[...])
    qn_ref[...] = q[:, 0:WIDTH].astype(BF16)
    rope_w = HEADS * LANES
    for h in range(HEADS):
        plain = q[:, WIDTH + h * LANES:WIDTH + (h + 1) * LANES]
        swapped = q[:, WIDTH + rope_w + h * LANES:WIDTH + rope_w + (h + 1) * LANES]
        qr_ref[:, h * LANES:(h + 1) * LANES] = (plain * cos + swapped * sin).astype(BF16)

    kv = _dot(_rms_norm(c_kv, kvg_ref[...]).astype(BF16), wkv_ref[...])
    k_rope = (k_r * cos + k_r_swapped * sin).astype(BF16)
    for g, k_ref in enumerate((k01_ref, k23_ref)):
        k_ref[:, 0:LANES] = kv[:, g * LANES:(g + 1) * LANES].astype(BF16)
        k_ref[:, LANES:] = k_rope
    v_ref[...] = kv[:, WIDTH:].astype(BF16)


def _swap_halves(w):
    half = w.shape[-1] // 2
    return jnp.concatenate([w[..., half:], w[..., :half]], axis=-1)


def _pad_cols(w, width):
    return jnp.pad(w, ((0, 0), (0, width - w.shape[1])))


def _take_cols_kernel(a_ref, b_ref, o_ref, *, shift):
    both = jnp.concatenate([a_ref[...], b_ref[...]], axis=1)
    o_ref[...] = both[:, shift:shift + o_ref.shape[1]].astype(BF16)


def _take_cols(w, start, width, block):
    rows = w.shape[0]
    first, shift = divmod(start, block)
    return pl.pallas_call(
        functools.partial(_take_cols_kernel, shift=shift),
        grid=(width // block,),
        in_specs=[pl.BlockSpec((rows, block), lambda j: (0, j + first)),
                  pl.BlockSpec((rows, block), lambda j: (0, j + first + 1))],
        out_specs=pl.BlockSpec((rows, block), lambda j: (0, j)),
        out_shape=jax.ShapeDtypeStruct((rows, width), BF16),
        compiler_params=_cparams("parallel"),
        name="take_cols",
    )(w, w)


def _mla_proj(h16, w_cq, w_ckv, w_kr, q_norm_g, kv_norm_g, w_uq, w_ukv, cos_t, sin_t):
    bsz, lp, d = h16.shape
    w_in = jnp.concatenate([w_cq, w_ckv, _pad_cols(w_kr, LANES), _pad_cols(_swap_halves(w_kr), LANES)], axis=1)
    uq = w_uq.reshape(MLA_Q_RANK, HEADS, MLA_NOPE + MLA_ROPE)
    q_nope = uq[:, :, :MLA_NOPE].reshape(MLA_Q_RANK, WIDTH)
    q_rope = uq[:, :, MLA_NOPE:]
    lane_pad = lambda r: jnp.pad(r, ((0, 0), (0, 0), (0, LANES - MLA_ROPE))).reshape(MLA_Q_RANK, HEADS * LANES)
    wq = jnp.concatenate([q_nope, lane_pad(q_rope), lane_pad(_swap_halves(q_rope))], axis=1)
    ukv = w_ukv.reshape(MLA_KV_RANK, HEADS, MLA_NOPE + HEAD_DIM)
    wkv = jnp.concatenate([ukv[:, :, :MLA_NOPE].reshape(MLA_KV_RANK, WIDTH),
                           ukv[:, :, MLA_NOPE:].reshape(MLA_KV_RANK, WIDTH)], axis=1)
    table = pl.BlockSpec((ROW_TILE, LANES), lambda b, t: (t, 0))
    act = lambda w: jax.ShapeDtypeStruct((bsz, lp, w), BF16)
    return pl.pallas_call(
        _mla_proj_kernel,
        grid=(bsz, lp // ROW_TILE),
        in_specs=[_rows(ROW_TILE, d), _whole(w_in.shape), _whole((1, MLA_Q_RANK)), _whole((1, MLA_KV_RANK)),
                  _whole(wq.shape), _whole(wkv.shape), table, table],
        out_specs=[_rows(ROW_TILE, WIDTH), _rows(ROW_TILE, HEADS * LANES),
                   _rows(ROW_TILE, WIDTH), _rows(ROW_TILE, WIDTH), _rows(ROW_TILE, WIDTH)],
        out_shape=[act(WIDTH), act(HEADS * LANES), act(WIDTH), act(WIDTH), act(WIDTH)],
        compiler_params=_cparams("parallel", "parallel"),
        name="mla_proj",
    )(h16, w_in.astype(BF16), q_norm_g.reshape(1, -1), kv_norm_g.reshape(1, -1),
      wq.astype(BF16), wkv.astype(BF16), cos_t, sin_t)


def _rotary_tables(lp):
    inv = 1.0 / (ROPE_BASE ** (jnp.arange(0, MLA_ROPE, 2, dtype=F32) / MLA_ROPE))
    ang = jnp.arange(lp, dtype=F32)[:, None] * inv[None, :]
    cos, sin = jnp.cos(ang), jnp.sin(ang)
    pad = jnp.zeros((lp, LANES - MLA_ROPE), F32)
    return (jnp.concatenate([cos, cos, pad], axis=1), jnp.concatenate([-sin, sin, pad], axis=1))


def _attn_kernel(*refs, scale, has_rope, has_bias):
    it = iter(refs)
    q_ref = next(it)
    qr_ref = next(it) if has_rope else None
    k_refs = [next(it) for _ in range(2 if has_rope else 1)]
    v_ref = next(it)
    f_ref = next(it) if has_bias else None
    o_ref = next(it)
    m_ref, l_ref, acc_ref = next(it), next(it), next(it)

    tq = q_ref.shape[0]
    qi = pl.program_id(1)
    heads_per_group = HEADS // len(k_refs)
    q_stacks = []
    for g in range(len(k_refs)):
        parts = []
        for h in range(g * heads_per_group, (g + 1) * heads_per_group):
            if has_rope:
                qg = q_ref[:, g * LANES:(g + 1) * LANES]
                qh = jnp.where(_head_mask(qg.shape, 1, h % heads_per_group), qg, jnp.zeros_like(qg))
                qh = jnp.concatenate([qh, qr_ref[:, h * LANES:(h + 1) * LANES]], axis=1)
            else:
                q = q_ref[...]
                qh = jnp.where(_head_mask(q.shape, 1, h), q, jnp.zeros_like(q))
            parts.append(qh)
        q_stacks.append(jnp.concatenate(parts, axis=0))

    m_ref[...] = jnp.full(m_ref.shape, NEG_BIG, F32)
    l_ref[...] = jnp.zeros_like(l_ref)
    acc_ref[...] = jnp.zeros_like(acc_ref)
    c = scale * LOG2E
    if has_bias:
        f_q0 = f_ref[:, pl.ds(pl.multiple_of(qi * tq, tq), LANES)][:, 0:1]

    def block(start, tk, masked):
        vb = v_ref[pl.ds(start, tk), :]
        scores = [_dot_nt(qs, k_ref[pl.ds(start, tk), :]) for qs, k_ref in zip(q_stacks, k_refs)]
        if has_bias:
            bias = (f_q0 - f_ref[:, pl.ds(start, tk)]) * LOG2E
        if masked:
            keep = _iota((tq, tk), 0) >= _iota((tq, tk), 1)
        for h in range(HEADS):
            rows = slice(h * tq, (h + 1) * tq)
            in_group = h % heads_per_group
            sh = scores[h // heads_per_group][in_group * tq:(in_group + 1) * tq] * c
            if has_bias:
                sh = sh + bias[h:h + 1, :]
            if masked:
                sh = jnp.where(keep, sh, NEG_BIG)
            m_old = m_ref[rows]
            m_new = jnp.maximum(m_old, jnp.max(sh, axis=1, keepdims=True))
            p = jnp.exp2(sh - _tile_lanes(m_new, tk // LANES))
            alpha = jnp.exp2(m_old - m_new)
            l_ref[rows] = alpha * l_ref[rows] + jnp.sum(p, axis=1, keepdims=True)
            m_ref[rows] = m_new
            acc_ref[rows] = acc_ref[rows] * _tile_lanes(alpha, WIDTH // LANES) + _dot(p.astype(BF16), vb)

    per = ATTN_KEYS // tq
    step_keys = ATTN_UNROLL * ATTN_KEYS
    n_steps = (qi * tq) // step_keys

    def body(j, carry):
        base = pl.multiple_of(j * step_keys, step_keys)
        for u in range(ATTN_UNROLL):
            block(base + u * ATTN_KEYS, ATTN_KEYS, False)
        return carry

    lax.fori_loop(0, n_steps, body, 0)
    left = qi - n_steps * (step_keys // tq)
    rest = pl.multiple_of(n_steps * step_keys, step_keys)
    for u in range(ATTN_UNROLL - 1):
        @pl.when(left >= (u + 1) * per)
        def _():
            block(rest + u * ATTN_KEYS, ATTN_KEYS, False)
    tail = left % per
    for r in range(per - 1):
        @pl.when(tail > r)
        def _():
            block(pl.multiple_of((qi - tail + r) * tq, tq), tq, False)

    block(pl.multiple_of(qi * tq, tq), tq, True)

    out = jnp.zeros((tq, WIDTH), F32)
    for h in range(HEADS):
        rows = slice(h * tq, (h + 1) * tq)
        oh = acc_ref[rows] / _tile_lanes(l_ref[rows], WIDTH // LANES)
        out = jnp.where(_head_mask(out.shape, 1, h), oh, out)
    o_ref[...] = out.astype(BF16)


def _attention(q, keys, v, scale, q_rope=None, f_rows=None):
    bsz, lp, _ = q.shape
    tq = ROW_TILE
    resident = lambda w: pl.BlockSpec((None, lp, w), lambda b, t: (b, 0, 0))
    in_specs = [_rows(tq, WIDTH)]
    args = [q]
    if q_rope is not None:
        in_specs.append(_rows(tq, HEADS * LANES))
        args.append(q_rope)
    in_specs += [resident(WIDTH)] * (len(keys) + 1)
    args += [*keys, v]
    if f_rows is not None:
        in_specs.append(pl.BlockSpec((None, SUBLANES, lp), lambda b, t: (b, 0, 0)))
        args.append(f_rows)
    kern = functools.partial(_attn_kernel, scale=scale, has_rope=q_rope is not None,
                             has_bias=f_rows is not None)
    return pl.pallas_call(
        kern,
        grid=(bsz, lp // tq),
        in_specs=in_specs,
        out_specs=_rows(tq, WIDTH),
        out_shape=jax.ShapeDtypeStruct((bsz, lp, WIDTH), BF16),
        scratch_shapes=[pltpu.VMEM((HEADS * tq, LANES), F32), pltpu.VMEM((HEADS * tq, LANES), F32),
                        pltpu.VMEM((HEADS * tq, WIDTH), F32)],
        compiler_params=_cparams("parallel", "arbitrary"),
        name="attn_bias" if f_rows is not None else "attn_rope",
    )(*args)


def _bf(x):
    return x.astype(BF16)


def _unit_lower_inverse(ms, sub_blk):
    n = ms[0].shape[0]
    eye = (_iota((n, n), 0) == _iota((n, n), 1)).astype(F32)
    md = [jnp.where(sub_blk, m, 0.0) for m in ms]
    mo = [_bf(m - d) for m, d in zip(ms, md)]
    x = [eye + d for d in md]
    p = md
    for _ in range(int(math.log2(RWKV_SUB)) - 1):
        p = [_dot(_bf(t), _bf(t)) for t in p]
        x = [_dot(_bf(a), _bf(eye + t)) for a, t in zip(x, p)]
    xb = [_bf(a) for a in x]
    pm = [_dot(a, o) for a, o in zip(xb, mo)]
    pmb = [_bf(t) for t in pm]
    pm2 = [_dot(t, t) for t in pmb]
    series = [eye + a + b + _dot(ab, _bf(b)) for a, b, ab in zip(pm, pm2, pmb)]
    return [_dot(_bf(t), a) for t, a in zip(series, xb)]


def _rwkv_kernel(h_ref, w_ref, mu_ref, w0_ref, w2_ref, a0_ref, a2_ref, g2_ref, kk_ref, ka_ref,
                 rk_ref, gng_ref, gnb_ref, ones_ref, o_ref, pcarry_ref, state_ref, y_ref):
    @pl.when(pl.program_id(1) == 0)
    def _():
        pcarry_ref[...] = jnp.zeros_like(pcarry_ref)
        state_ref[...] = jnp.zeros_like(state_ref)

    p_in = _dot(h_ref[...], w_ref[...])
    prev = _shift_rows(p_in, 1, pcarry_ref[...])
    pcarry_ref[...] = p_in[ROW_TILE - SUBLANES:, :]
    p = p_in + (prev - p_in) * mu_ref[...]
    r = p[:, 0:WIDTH]
    k = p[:, WIDTH:2 * WIDTH]
    v = p[:, 2 * WIDTH:3 * WIDTH]
    lora = p[:, 3 * WIDTH:]
    ones_bd = ones_ref[...]

    z = w0_ref[...] + _dot(jnp.tanh(lora).astype(BF16), w2_ref[...])
    log_w = (-RWKV_DECAY_SCALE) * _sigmoid(z)
    a_gate = _sigmoid(a0_ref[...] + _dot(lora.astype(BF16), a2_ref[...]))
    gate = _dot(_sigmoid(lora).astype(BF16), g2_ref[...])
    kk = k * kk_ref[...]
    kk = kk * lax.rsqrt(jnp.maximum(_head_sum(kk * kk, ones_bd), 1e-24))
    kf = k * (1.0 + (a_gate - 1.0) * ka_ref[...])
    a_vec = -kk
    b_vec = kk * a_gate

    n = HEADS * RWKV_CHUNK
    ri, ci = _iota((n, n), 0), _iota((n, n), 1)
    strict = ri > ci
    incl = ri >= ci
    sub_blk = (ri // RWKV_SUB) == (ci // RWKV_SUB)
    diag = ri == ci
    mid = RWKV_CHUNK // 2 - 1
    chunks = range(ROW_TILE // RWKV_CHUNK)
    chunk_rows = [slice(c * RWKV_CHUNK, (c + 1) * RWKV_CHUNK) for c in chunks]
    stack = lambda t: _bf(_stack_heads(t))

    a_t, r_t, b_t, k_t, a_0, r_0, b_e, k_e, v_s, w_end = ([] for _ in range(10))
    for rows in chunk_rows:
        lw = log_w[rows]
        g_in = _cumsum_rows(lw)
        g_ex = g_in - lw
        g_mid = g_in[mid:mid + 1, :]
        g_end = g_in[RWKV_CHUNK - 1:RWKV_CHUNK, :]
        decay_out = jnp.exp(g_mid - g_in)
        to_end = jnp.exp(g_end - g_in)
        a_c, b_c, k_c, r_c = a_vec[rows], b_vec[rows], kf[rows], r[rows]
        a_t.append(stack(a_c * jnp.exp(g_ex - g_mid)))
        r_t.append(stack(r_c * jnp.exp(g_in - g_mid)))
        b_t.append(stack(b_c * decay_out))
        k_t.append(stack(k_c * decay_out))
        a_0.append(stack(a_c * jnp.exp(g_ex)))
        r_0.append(_stack_heads(r_c * jnp.exp(g_in)))
        b_e.append(stack(b_c * to_end))
        k_e.append(stack(k_c * to_end))
        v_s.append(stack(v[rows]))
        w_end.append(jnp.exp(g_end))

    each = lambda f, *lists: [f(*args) for args in zip(*lists)]
    m_ab = each(lambda a, b: jnp.where(strict, _dot_nt(a, b), 0.0), a_t, b_t)
    m_ak = each(lambda a, b: _bf(jnp.where(strict, _dot_nt(a, b), 0.0)), a_t, k_t)
    m_rb = each(lambda a, b: _bf(jnp.where(incl, _dot_nt(a, b), 0.0)), r_t, b_t)
    m_rk = each(lambda a, b: _bf(jnp.where(incl, _dot_nt(a, b), 0.0)), r_t, k_t)
    t_inv = each(_bf, _unit_lower_inverse(m_ab, sub_blk))
    w_a = each(lambda t, a: _bf(_dot(t, a)), t_inv, a_0)
    m_v = each(lambda m, x: _bf(_dot(m, x)), m_ak, v_s)
    u_v = each(lambda t, x: _bf(_dot(t, x)), t_inv, m_v)
    r_eff = each(lambda r0, m, x: r0 + _dot(m, x), r_0, m_rb, w_a)
    y_v = each(lambda mb, u, mk, x: _dot(mb, u) + _dot(mk, x), m_rb, u_v, m_rk, v_s)
    g_mat = each(lambda w, b, x: jnp.where(diag, w, 0.0) + _dot_tn(b, x), w_end, b_e, w_a)
    d_mat = each(lambda b, u, kx, x: _dot_tn(b, u) + _dot_tn(kx, x), b_e, u_v, k_e, v_s)

    for c, rows in enumerate(chunk_rows):
        state = _split(state_ref[...], 2)
        y_ref[rows, :] = _unstack_heads(_mdot(_dot, _split(r_eff[c], 2), state) + y_v[c])
        state_ref[...] = _mdot(_dot, _split(g_mat[c], 2), state) + d_mat[c]

    y = y_ref[...]
    inv_n = 1.0 / HEAD_DIM
    y_mu = _head_sum(y, ones_bd) * inv_n
    yc = y - y_mu
    y_var = _head_sum(yc * yc, ones_bd) * inv_n
    y = yc * lax.rsqrt(y_var + RWKV_GN_EPS) * gng_ref[...] + gnb_ref[...]
    y = y + _head_sum(r * kf * rk_ref[...], ones_bd) * v
    o_ref[...] = (y * gate).astype(BF16)


def _rwkv_branch(h16, w_p, mu, w0, w2, a0, a2, g2, k_k, k_a, r_k, gn_g, gn_b):
    bsz, lp, d = h16.shape
    row = lambda t: t.reshape(1, -1)
    lora_rows = lambda w, lo: jnp.zeros((RWKV_LORA, WIDTH), F32).at[lo:lo + w.shape[0]].set(w).astype(BF16)
    ones_bd = _block_diag(jnp.ones((HEADS, HEAD_DIM, HEAD_DIM), F32)).astype(BF16)
    params = [w_p.astype(BF16), row(mu), row(w0), lora_rows(w2, 0), row(a0), lora_rows(a2, 32),
              lora_rows(g2, 64), row(k_k), row(k_a), row(r_k), row(gn_g), row(gn_b), ones_bd]
    return pl.pallas_call(
        _rwkv_kernel,
        grid=(bsz, lp // ROW_TILE),
        in_specs=[_rows(ROW_TILE, d)] + [_whole(t.shape) for t in params],
        out_specs=_rows(ROW_TILE, WIDTH),
        out_shape=jax.ShapeDtypeStruct((bsz, lp, WIDTH), BF16),
        scratch_shapes=[pltpu.VMEM((SUBLANES, RWKV_IN), F32), pltpu.VMEM((WIDTH, WIDTH), F32),
                        pltpu.VMEM((ROW_TILE, WIDTH), F32)],
        compiler_params=_cparams("parallel", "arbitrary"),
        name="rwkv",
    )(h16, *params)


def _merge_kernel(h32_ref, h16_ref, oa_ref, ob_ref, oc_ref, od_ref, wg_ref, wb_ref, wo_ref, g_ref, b_ref,
                  o32_ref, o16_ref, *, alpha):
    h16 = h16_ref[...]
    acc = None
    for n, o_ref in enumerate((oa_ref, ob_ref, oc_ref, od_ref)):
        gate = _sigmoid(_dot(h16, wg_ref[:, n * D_MODEL:(n + 1) * D_MODEL]))
        term = gate * _dot(o_ref[...], wb_ref[n])
        acc = term if acc is None else acc + term
    mixed = _dot(acc.astype(BF16), wo_ref[...])
    y = _layer_norm(alpha * h32_ref[...] + mixed, g_ref[...], b_ref[...])
    o32_ref[...] = y
    o16_ref[...] = y.astype(BF16)


def _merge(h32, h16, branches, w_gate, w_branch, w_out, ln_g, ln_b, alpha):
    bsz, lp, d = h32.shape
    return pl.pallas_call(
        functools.partial(_merge_kernel, alpha=alpha),
        grid=(bsz, lp // ROW_TILE),
        in_specs=[_rows(ROW_TILE, d), _rows(ROW_TILE, d)] + [_rows(ROW_TILE, WIDTH)] * N_BRANCH
        + [_whole((d, N_BRANCH * d)), _whole((N_BRANCH, WIDTH, d)), _whole((d, d)), _whole((1, d)), _whole((1, d))],
        out_specs=[_rows(ROW_TILE, d), _rows(ROW_TILE, d)],
        out_shape=[jax.ShapeDtypeStruct(h32.shape, F32), jax.ShapeDtypeStruct(h32.shape, BF16)],
        compiler_params=_cparams("parallel", "parallel"),
        name="merge",
    )(h32, h16, *branches, w_gate.astype(BF16), w_branch.astype(BF16), w_out.astype(BF16),
      ln_g.reshape(1, d), ln_b.reshape(1, d))


def _ffn_kernel(h32_ref, h16_ref, wu_ref, cw_ref, cb_ref, wd_ref, g_ref, b_ref,
                o32_ref, o16_ref, carry_ref, act_ref, *, alpha):
    @pl.when(pl.program_id(1) == 0)
    def _():
        carry_ref[...] = jnp.zeros_like(carry_ref)

    h16 = h16_ref[...]

    def conv_cols(lo):
        cols = slice(lo, lo + FFN_CHUNK)
        x = _dot(h16, wu_ref[:, cols])
        carry = carry_ref[:, cols]
        out = cw_ref[FFN_CONV - 1:FFN_CONV, cols] * x + cb_ref[:, cols]
        for d in range(1, FFN_CONV):
            k = FFN_CONV - 1 - d
            out = out + cw_ref[k:k + 1, cols] * _shift_rows(x, d, carry)
        carry_ref[:, cols] = x[ROW_TILE - SUBLANES:, :]
        return out

    for j in range(D_FF // FFN_CHUNK):
        gate = conv_cols(j * FFN_CHUNK)
        val = conv_cols(D_FF + j * FFN_CHUNK)
        act_ref[:, j * FFN_CHUNK:(j + 1) * FFN_CHUNK] = (gate * _sigmoid(gate) * val).astype(BF16)
    y = _layer_norm(alpha * h32_ref[...] + _dot(act_ref[...], wd_ref[...]), g_ref[...], b_ref[...])
    o32_ref[...] = y
    o16_ref[...] = y.astype(BF16)


def _ffn(h32, h16, w_up, conv_w, conv_b, w_down, ln_g, ln_b, alpha):
    bsz, lp, d = h32.shape
    params = [w_up.astype(BF16), jnp.pad(conv_w, ((0, SUBLANES - FFN_CONV), (0, 0))), conv_b.reshape(1, -1),
              w_down.astype(BF16), ln_g.reshape(1, d), ln_b.reshape(1, d)]
    return pl.pallas_call(
        functools.partial(_ffn_kernel, alpha=alpha),
        grid=(bsz, lp // ROW_TILE),
        in_specs=[_rows(ROW_TILE, d), _rows(ROW_TILE, d)] + [_whole(t.shape) for t in params],
        out_specs=[_rows(ROW_TILE, d), _rows(ROW_TILE, d)],
        out_shape=[jax.ShapeDtypeStruct(h32.shape, F32), jax.ShapeDtypeStruct(h32.shape, BF16)],
        scratch_shapes=[pltpu.VMEM((SUBLANES, 2 * D_FF), F32), pltpu.VMEM((ROW_TILE, D_FF), BF16)],
        compiler_params=_cparams("parallel", "arbitrary"),
        name="ffn",
    )(h32, h16, *params)


def kernel(x, meta_tokens, ln_in_g, ln_in_b, w_in, w_branch, w_out, ln_mix_g, ln_mix_b, lru_conv_w, lru_conv_b, lru_w_rg, lru_b_rg, lru_w_ig, lru_b_ig, lru_lambda, fox_b_f, mla_q_norm_g, mla_kv_norm_g, mla_w_uq, mla_w_ukv, rwkv_mu, rwkv_w0, rwkv_w2, rwkv_a0, rwkv_a2, rwkv_g2, rwkv_k_k, rwkv_k_a, rwkv_r_k, rwkv_gn_g, rwkv_gn_b, ffn_w_up, ffn_conv_w, ffn_conv_b, ffn_w_down, ln_ffn_g, ln_ffn_b):
    bsz, seq, d = x.shape
    depth = w_in.shape[0]
    alpha = (2.0 * depth) ** 0.25
    t_real = N_META + seq
    lp = -(-t_real // ROW_TILE) * ROW_TILE
    meta = jnp.broadcast_to(meta_tokens.astype(x.dtype)[None], (bsz, N_META, d))
    tokens = jnp.concatenate([meta, x, jnp.zeros((bsz, lp - t_real, d), x.dtype)], axis=1)
    h32, h16 = _ln_in(tokens, ln_in_g, ln_in_b)
    cos_t, sin_t = _rotary_tables(lp)

    widths = (LRU_WIDTH, LRU_WIDTH, WIDTH, WIDTH, WIDTH, HEADS, MLA_Q_RANK, MLA_KV_RANK, MLA_ROPE,
              RWKV_IN, N_BRANCH * D_MODEL)
    offs = np.concatenate([[0], np.cumsum(widths)])
    col = lambda w, i, j=None: w[:, int(offs[i]):int(offs[(i if j is None else j) + 1])]

    for l in range(depth):
        wl = w_in[l]
        o_a = _lru_branch(h16, col(wl, 0, 1), lru_conv_w[l], lru_conv_b[l], lru_w_rg[l], lru_b_rg[l],
                          lru_w_ig[l], lru_b_ig[l], lru_lambda[l])
        w_fox = jnp.concatenate([col(wl, 2, 4), _pad_cols(col(wl, 5), LANES)], axis=1)
        fq, fk, fv, f_cum = _fox_proj(h16, w_fox, fox_b_f[l])
        f_rows = jnp.swapaxes(f_cum[:, :, :SUBLANES], 1, 2)
        o_b = _attention(fq, (fk,), fv, 1.0 / math.sqrt(HEAD_DIM), f_rows=f_rows)
        w_mla = _take_cols(wl, int(offs[6]), 4 * LANES, 4 * LANES)
        w_cq, w_ckv = w_mla[:, :MLA_Q_RANK], w_mla[:, MLA_Q_RANK:MLA_Q_RANK + MLA_KV_RANK]
        w_kr = w_mla[:, MLA_Q_RANK + MLA_KV_RANK:MLA_Q_RANK + MLA_KV_RANK + MLA_ROPE]
        w_tail = _take_cols(wl, int(offs[9]), RWKV_IN + N_BRANCH * D_MODEL, 3 * LANES)
        mq, mqr, mk01, mk23, mv = _mla_proj(h16, w_cq, w_ckv, w_kr, mla_q_norm_g[l],
                                            mla_kv_norm_g[l], mla_w_uq[l], mla_w_ukv[l], cos_t, sin_t)
        o_c = _attention(mq, (mk01, mk23), mv, 1.0 / math.sqrt(MLA_NOPE + MLA_ROPE), q_rope=mqr)
        o_d = _rwkv_branch(h16, w_tail[:, :RWKV_IN], rwkv_mu[l], rwkv_w0[l], rwkv_w2[l], rwkv_a0[l], rwkv_a2[l],
                           rwkv_g2[l], rwkv_k_k[l], rwkv_k_a[l], rwkv_r_k[l], rwkv_gn_g[l], rwkv_gn_b[l])
        h32, h16 = _merge(h32, h16, (o_a, o_b, o_c, o_d), w_tail[:, RWKV_IN:], w_branch[l], w_out[l],
                          ln_mix_g[l], ln_mix_b[l], alpha)
        h32, h16 = _ffn(h32, h16, ffn_w_up[l], ffn_conv_w[l], ffn_conv_b[l], ffn_w_down[l],
                        ln_ffn_g[l], ln_ffn_b[l], alpha)
    return h32[:, N_META:t_real]
```

```python
import functools
import math

import jax
import jax.numpy as jnp
import numpy as np
from jax import lax
from jax.experimental import pallas as pl
from jax.experimental.pallas import tpu as pltpu

F32 = jnp.float32
BF16 = jnp.bfloat16

D_MODEL = 1024
N_META = 16
LN_EPS = 1e-5
RMS_EPS = 1e-6
NEG_BIG = -1e30
LOG2E = 1.4426950408889634

LRU_WIDTH = 256
LRU_BLOCKS = 4
LRU_CONV = 4
LRU_C = 8.0
HEADS = 4
HEAD_DIM = 64
WIDTH = HEADS * HEAD_DIM
MLA_Q_RANK = 256
MLA_KV_RANK = 128
MLA_NOPE = 64
MLA_ROPE = 32
ROPE_BASE = 10000.0
RWKV_LORA = 128
RWKV_IN = 3 * WIDTH + RWKV_LORA
RWKV_GN_EPS = 64e-5
RWKV_DECAY_SCALE = math.exp(-0.5)
N_BRANCH = 4
D_FF = 2816
FFN_CONV = 3

SUBLANES = 8
LANES = 128
MXU_DIM = 256
ROW_TILE = 256
RWKV_CHUNK = 64
RWKV_SUB = 16
FFN_CHUNK = 256
ATTN_KEYS = 1024
ATTN_UNROLL = 1
VMEM_LIMIT = 56 * 1024 * 1024


def _cparams(*sem):
    return pltpu.CompilerParams(dimension_semantics=sem, vmem_limit_bytes=VMEM_LIMIT)


def _whole(shape):
    nd = len(shape)
    return pl.BlockSpec(shape, lambda *_: (0,) * nd)


def _rows(tile, width):
    return pl.BlockSpec((None, tile, width), lambda b, t: (b, t, 0))


def _iota(shape, dim):
    return lax.broadcasted_iota(jnp.int32, shape, dim)


def _dot(a, b):
    return jnp.dot(a, b, preferred_element_type=F32)


def _dot_nt(a, b):
    return lax.dot_general(a, b, (((1,), (1,)), ((), ())), preferred_element_type=F32)


def _dot_tn(a, b):
    return lax.dot_general(a, b, (((0,), (0,)), ((), ())), preferred_element_type=F32)


def _split(x, terms):
    out = []
    for _ in range(terms - 1):
        hi = x.astype(BF16)
        out.append(hi)
        x = x - hi.astype(F32)
    out.append(x.astype(BF16))
    return out


def _mdot(dot, a_terms, b_terms):
    order = max(len(a_terms), len(b_terms))
    acc = None
    for i, a in enumerate(a_terms):
        for j, b in enumerate(b_terms):
            if i + j < order:
                p = dot(a, b)
                acc = p if acc is None else acc + p
    return acc


def _sigmoid(x):
    return 1.0 / (1.0 + jnp.exp(-x))


def _softplus(x):
    return jnp.maximum(x, 0.0) + jnp.log(1.0 + jnp.exp(-jnp.abs(x)))


def _gelu_tanh(x):
    return 0.5 * x * (1.0 + jnp.tanh(math.sqrt(2.0 / math.pi) * (x + 0.044715 * (x * x * x))))


def _layer_norm(x, g, b):
    mu = jnp.mean(x, -1, keepdims=True)
    xc = x - mu
    var = jnp.mean(xc * xc, -1, keepdims=True)
    return xc * lax.rsqrt(var + LN_EPS) * g + b


def _shift_rows(x, d, carry):
    rolled = pltpu.roll(x, d, 0)
    head = rolled[:SUBLANES]
    rows = _iota(head.shape, 0)
    for r in range(d):
        src = SUBLANES - d + r
        head = jnp.where(rows == r, carry[src:src + 1, :], head)
    return jnp.concatenate([head, rolled[SUBLANES:]], axis=0)


def _cumsum_rows(x):
    rows = _iota(x.shape, 0)
    d = 1
    while d < x.shape[0]:
        x = x + jnp.where(rows >= d, pltpu.roll(x, d, 0), 0.0)
        d *= 2
    return x


def _linear_scan_rows(a, u):
    rows = _iota(a.shape, 0)
    d = 1
    while d < a.shape[0]:
        keep = rows >= d
        u = u + a * jnp.where(keep, pltpu.roll(u, d, 0), 0.0)
        a = a * jnp.where(keep, pltpu.roll(a, d, 0), 1.0)
        d *= 2
    return a, u


def _tile_lanes(x, n):
    return jnp.concatenate([x] * n, axis=1)


def _head_mask(shape, lane_dim, head):
    lane = _iota(shape, lane_dim)
    return (lane >= head * HEAD_DIM) & (lane < (head + 1) * HEAD_DIM)


def _stack_heads(x):
    return jnp.concatenate(
        [jnp.where(_head_mask(x.shape, 1, h), x, jnp.zeros_like(x)) for h in range(HEADS)], axis=0)


def _head_sum(x, ones_bd):
    return _mdot(_dot, _split(x, 2), [ones_bd])


def _ln_in_kernel(x_ref, g_ref, b_ref, o32_ref, o16_ref):
    y = _layer_norm(x_ref[...], g_ref[...], b_ref[...])
    o32_ref[...] = y
    o16_ref[...] = y.astype(BF16)


def _ln_in(x, g, b):
    bsz, lp, d = x.shape
    return pl.pallas_call(
        _ln_in_kernel,
        grid=(bsz, lp // ROW_TILE),
        in_specs=[_rows(ROW_TILE, d), _whole((1, d)), _whole((1, d))],
        out_specs=[_rows(ROW_TILE, d), _rows(ROW_TILE, d)],
        out_shape=[jax.ShapeDtypeStruct(x.shape, F32), jax.ShapeDtypeStruct(x.shape, BF16)],
        compiler_params=_cparams("parallel", "parallel"),
        name="ln_in",
    )(x, g.reshape(1, d), b.reshape(1, d))


def _lru_kernel(h_ref, w_ref, cw_ref, cb_ref, wrg_ref, brg_ref, wig_ref, big_ref, lam_ref,
                o_ref, xcarry_ref, hcarry_ref):
    @pl.when(pl.program_id(1) == 0)
    def _():
        xcarry_ref[...] = jnp.zeros_like(xcarry_ref)
        hcarry_ref[...] = jnp.zeros_like(hcarry_ref)

    yx = _dot(h_ref[...], w_ref[...])
    y = yx[:, :LRU_WIDTH]
    x = yx[:, LRU_WIDTH:]
    xcarry = xcarry_ref[...]
    xc = cw_ref[LRU_CONV - 1:LRU_CONV, :] * x + cb_ref[...]
    for d in range(1, LRU_CONV):
        k = LRU_CONV - 1 - d
        xc = xc + cw_ref[k:k + 1, :] * _shift_rows(x, d, xcarry)
    xcarry_ref[...] = x[ROW_TILE - SUBLANES:, :]

    xc16 = xc.astype(BF16)
    r = _sigmoid(_dot(xc16, wrg_ref[...]) + brg_ref[...])
    i = _sigmoid(_dot(xc16, wig_ref[...]) + big_ref[...])
    log_a = (-LRU_C) * r * _softplus(-lam_ref[...])
    a = jnp.exp(log_a)
    th = jnp.tanh(log_a)
    u = jnp.sqrt(-2.0 * th / (1.0 - th)) * (i * xc)
    a_run, h = _linear_scan_rows(a, u)
    h = h + a_run * hcarry_ref[SUBLANES - 1:SUBLANES, :]
    hcarry_ref[...] = h[ROW_TILE - SUBLANES:, :]
    o_ref[...] = (_gelu_tanh(y) * h).astype(BF16)


def _block_diag(w):
    n, d, e = w.shape
    eye = jnp.eye(n, dtype=w.dtype)
    return (eye[:, None, :, None] * w[:, :, None, :]).reshape(n * d, n * e)


def _lru_branch(h16, w_yx, conv_w, conv_b, w_rg, b_rg, w_ig, b_ig, lam):
    bsz, lp, d = h16.shape
    row = lambda v: v.reshape(1, LRU_WIDTH)
    return pl.pallas_call(
        _lru_kernel,
        grid=(bsz, lp // ROW_TILE),
        in_specs=[_rows(ROW_TILE, d), _whole((d, 2 * LRU_WIDTH)),
                  _whole((LRU_CONV, LRU_WIDTH)), _whole((1, LRU_WIDTH)),
                  _whole((LRU_WIDTH, LRU_WIDTH)), _whole((1, LRU_WIDTH)),
                  _whole((LRU_WIDTH, LRU_WIDTH)), _whole((1, LRU_WIDTH)),
                  _whole((1, LRU_WIDTH))],
        out_specs=_rows(ROW_TILE, LRU_WIDTH),
        out_shape=jax.ShapeDtypeStruct((bsz, lp, LRU_WIDTH), BF16),
        scratch_shapes=[pltpu.VMEM((SUBLANES, LRU_WIDTH), F32), pltpu.VMEM((SUBLANES, LRU_WIDTH), F32)],
        compiler_params=_cparams("parallel", "arbitrary"),
        name="lru",
    )(h16, w_yx.astype(BF16), conv_w, row(conv_b), _block_diag(w_rg).astype(BF16), row(b_rg),
      _block_diag(w_ig).astype(BF16), row(b_ig), row(lam))


def _values_with_ones(v_spread):
    lane = _iota(v_spread.shape, 1)
    return jnp.where(lane % LANES >= HEAD_DIM, 1.0, v_spread).astype(BF16)


def _spread_heads(w):
    k = w.shape[0]
    return jnp.pad(w.reshape(k, HEADS, HEAD_DIM), ((0, 0), (0, 0), (0, LANES - HEAD_DIM))).reshape(k, HEADS * LANES)


def _fox_proj_kernel(h_ref, w_ref, bf_ref, q_ref, k_ref, v_ref, f_ref, fcarry_ref, *, q_scale):
    @pl.when(pl.program_id(1) == 0)
    def _():
        fcarry_ref[...] = jnp.zeros_like(fcarry_ref)

    z = _dot(h_ref[...], w_ref[...])
    q_ref[...] = (z[:, 0:WIDTH] * q_scale).astype(BF16)
    k_ref[...] = z[:, WIDTH:2 * WIDTH].astype(BF16)
    v_ref[...] = _values_with_ones(z[:, 2 * WIDTH:2 * WIDTH + HEADS * LANES])
    log_f = -_softplus(-(z[:, 2 * WIDTH + HEADS * LANES:] + bf_ref[...]))
    f = _cumsum_rows(log_f) + fcarry_ref[SUBLANES - 1:SUBLANES, :]
    fcarry_ref[...] = f[ROW_TILE - SUBLANES:, :]
    f_ref[...] = f


def _fox_proj(h16, w_qkvf, b_f, q_scale):
    bsz, lp, d = h16.shape
    bf = jnp.zeros((1, LANES), F32).at[0, :HEADS].set(b_f)
    act = lambda w: jax.ShapeDtypeStruct((bsz, lp, w), BF16)
    return pl.pallas_call(
        functools.partial(_fox_proj_kernel, q_scale=q_scale),
        grid=(bsz, lp // ROW_TILE),
        in_specs=[_rows(ROW_TILE, d), _whole(w_qkvf.shape), _whole((1, LANES))],
        out_specs=[_rows(ROW_TILE, WIDTH), _rows(ROW_TILE, WIDTH), _rows(ROW_TILE, HEADS * LANES),
                   _rows(ROW_TILE, LANES)],
        out_shape=[act(WIDTH), act(WIDTH), act(HEADS * LANES), jax.ShapeDtypeStruct((bsz, lp, LANES), F32)],
        scratch_shapes=[pltpu.VMEM((SUBLANES, LANES), F32)],
        compiler_params=_cparams("parallel", "arbitrary"),
        name="fox_proj",
    )(h16, w_qkvf.astype(BF16), bf)


def _rms_norm(x, g):
    return x * lax.rsqrt(jnp.mean(x * x, -1, keepdims=True) + RMS_EPS) * g


def _mla_proj_kernel(h_ref, win_ref, qg_ref, kvg_ref, wq_ref, wkv_ref, cos_ref, sin_ref,
                     qn_ref, qr_ref, k01_ref, k23_ref, v_ref, *, q_scale):
    z = _dot(h_ref[...], win_ref[...])
    c_q = z[:, 0:MLA_Q_RANK]
    c_kv = z[:, MLA_Q_RANK:MLA_Q_RANK + MLA_KV_RANK]
    k_r = z[:, MLA_Q_RANK + MLA_KV_RANK:MLA_Q_RANK + MLA_KV_RANK + LANES]
    k_r_swapped = z[:, MLA_Q_RANK + MLA_KV_RANK + LANES:]
    cos = cos_ref[...]
    sin = sin_ref[...]

    q = _dot(_rms_norm(c_q, qg_ref[...]).astype(BF16), wq_ref[...])
    qn_ref[...] = (q[:, 0:WIDTH] * q_scale).astype(BF16)
    rope_w = HEADS * LANES
    for h in range(HEADS):
        plain = q[:, WIDTH + h * LANES:WIDTH + (h + 1) * LANES]
        swapped = q[:, WIDTH + rope_w + h * LANES:WIDTH + rope_w + (h + 1) * LANES]
        qr_ref[:, h * LANES:(h + 1) * LANES] = ((plain * cos + swapped * sin) * q_scale).astype(BF16)

    kv = _dot(_rms_norm(c_kv, kvg_ref[...]).astype(BF16), wkv_ref[...])
    k_rope = (k_r * cos + k_r_swapped * sin).astype(BF16)
    for g, k_ref in enumerate((k01_ref, k23_ref)):
        k_ref[:, 0:LANES] = kv[:, g * LANES:(g + 1) * LANES].astype(BF16)
        k_ref[:, LANES:] = k_rope
    v_ref[...] = _values_with_ones(kv[:, WIDTH:])


def _swap_halves(w):
    half = w.shape[-1] // 2
    return jnp.concatenate([w[..., half:], w[..., :half]], axis=-1)


def _pad_cols(w, width):
    return jnp.pad(w, ((0, 0), (0, width - w.shape[1])))


def _take_cols_kernel(a_ref, b_ref, o_ref, *, shift):
    both = jnp.concatenate([a_ref[...], b_ref[...]], axis=1)
    o_ref[...] = both[:, shift:shift + o_ref.shape[1]].astype(BF16)


def _take_cols(w, start, width, block):
    rows = w.shape[0]
    first, shift = divmod(start, block)
    return pl.pallas_call(
        functools.partial(_take_cols_kernel, shift=shift),
        grid=(width // block,),
        in_specs=[pl.BlockSpec((rows, block), lambda j: (0, j + first)),
                  pl.BlockSpec((rows, block), lambda j: (0, j + first + 1))],
        out_specs=pl.BlockSpec((rows, block), lambda j: (0, j)),
        out_shape=jax.ShapeDtypeStruct((rows, width), BF16),
        compiler_params=_cparams("parallel"),
        name="take_cols",
    )(w, w)


def _mla_proj(h16, w_cq, w_ckv, w_kr, q_norm_g, kv_norm_g, w_uq, w_ukv, cos_t, sin_t, q_scale):
    bsz, lp, d = h16.shape
    w_in = jnp.concatenate([w_cq, w_ckv, _pad_cols(w_kr, LANES), _pad_cols(_swap_halves(w_kr), LANES)], axis=1)
    uq = w_uq.reshape(MLA_Q_RANK, HEADS, MLA_NOPE + MLA_ROPE)
    q_nope = uq[:, :, :MLA_NOPE].reshape(MLA_Q_RANK, WIDTH)
    q_rope = uq[:, :, MLA_NOPE:]
    lane_pad = lambda r: jnp.pad(r, ((0, 0), (0, 0), (0, LANES - MLA_ROPE))).reshape(MLA_Q_RANK, HEADS * LANES)
    wq = jnp.concatenate([q_nope, lane_pad(q_rope), lane_pad(_swap_halves(q_rope))], axis=1)
    ukv = w_ukv.reshape(MLA_KV_RANK, HEADS, MLA_NOPE + HEAD_DIM)
    wkv = jnp.concatenate([ukv[:, :, :MLA_NOPE].reshape(MLA_KV_RANK, WIDTH),
                           _spread_heads(ukv[:, :, MLA_NOPE:].reshape(MLA_KV_RANK, WIDTH))], axis=1)
    table = pl.BlockSpec((ROW_TILE, LANES), lambda b, t: (t, 0))
    act = lambda w: jax.ShapeDtypeStruct((bsz, lp, w), BF16)
    return pl.pallas_call(
        functools.partial(_mla_proj_kernel, q_scale=q_scale),
        grid=(bsz, lp // ROW_TILE),
        in_specs=[_rows(ROW_TILE, d), _whole(w_in.shape), _whole((1, MLA_Q_RANK)), _whole((1, MLA_KV_RANK)),
                  _whole(wq.shape), _whole(wkv.shape), table, table],
        out_specs=[_rows(ROW_TILE, WIDTH), _rows(ROW_TILE, HEADS * LANES),
                   _rows(ROW_TILE, WIDTH), _rows(ROW_TILE, WIDTH), _rows(ROW_TILE, HEADS * LANES)],
        out_shape=[act(WIDTH), act(HEADS * LANES), act(WIDTH), act(WIDTH), act(HEADS * LANES)],
        compiler_params=_cparams("parallel", "parallel"),
        name="mla_proj",
    )(h16, w_in.astype(BF16), q_norm_g.reshape(1, -1), kv_norm_g.reshape(1, -1),
      wq.astype(BF16), wkv.astype(BF16), cos_t, sin_t)


def _rotary_tables(lp):
    inv = 1.0 / (ROPE_BASE ** (jnp.arange(0, MLA_ROPE, 2, dtype=F32) / MLA_ROPE))
    ang = jnp.arange(lp, dtype=F32)[:, None] * inv[None, :]
    cos, sin = jnp.cos(ang), jnp.sin(ang)
    pad = jnp.zeros((lp, LANES - MLA_ROPE), F32)
    return (jnp.concatenate([cos, cos, pad], axis=1), jnp.concatenate([-sin, sin, pad], axis=1))


def _attn_kernel(*refs, has_rope, has_bias):
    it = iter(refs)
    q_ref = next(it)
    qr_ref = next(it) if has_rope else None
    k_refs = [next(it) for _ in range(2 if has_rope else 1)]
    v_ref = next(it)
    f_ref = next(it) if has_bias else None
    o_ref = next(it)
    m_ref, acc_ref = next(it), next(it)

    tq = q_ref.shape[0]
    qi = pl.program_id(1)
    heads_per_group = HEADS // len(k_refs)
    q_stacks = []
    for g in range(len(k_refs)):
        parts = []
        for h in range(g * heads_per_group, (g + 1) * heads_per_group):
            if has_rope:
                qg = q_ref[:, g * LANES:(g + 1) * LANES]
                qh = jnp.where(_head_mask(qg.shape, 1, h % heads_per_group), qg, jnp.zeros_like(qg))
                qh = jnp.concatenate([qh, qr_ref[:, h * LANES:(h + 1) * LANES]], axis=1)
            else:
                q = q_ref[...]
                qh = jnp.where(_head_mask(q.shape, 1, h), q, jnp.zeros_like(q))
            parts.append(qh)
        q_stacks.append(jnp.concatenate(parts, axis=0))

    m_ref[...] = jnp.full(m_ref.shape, NEG_BIG, F32)
    acc_ref[...] = jnp.zeros_like(acc_ref)
    if has_bias:
        f_q0 = f_ref[:, pl.ds(pl.multiple_of(qi * tq, tq), LANES)][:, 0:1]

    def block(start, tk, masked):
        scores = [_dot_nt(qs, k_ref[pl.ds(start, tk), :]) for qs, k_ref in zip(q_stacks, k_refs)]
        if has_bias:
            bias = (f_q0 - f_ref[:, pl.ds(start, tk)]) * LOG2E
        if masked:
            keep = _iota((tq, tk), 0) >= _iota((tq, tk), 1)
        for h in range(HEADS):
            rows = slice(h * tq, (h + 1) * tq)
            in_group = h % heads_per_group
            sh = scores[h // heads_per_group][in_group * tq:(in_group + 1) * tq]
            if has_bias:
                sh = sh + bias[h:h + 1, :]
            if masked:
                sh = jnp.where(keep, sh, NEG_BIG)
            m_old = m_ref[rows]
            m_new = jnp.maximum(m_old, jnp.max(sh, axis=1, keepdims=True))
            p = jnp.exp2(sh - _tile_lanes(m_new, tk // LANES))
            alpha = jnp.exp2(m_old - m_new)
            m_ref[rows] = m_new
            v_ones = v_ref[pl.ds(start, tk), h * LANES:(h + 1) * LANES]
            acc_ref[rows] = acc_ref[rows] * alpha + _dot(p.astype(BF16), v_ones)

    per = ATTN_KEYS // tq
    step_keys = ATTN_UNROLL * ATTN_KEYS
    n_steps = (qi * tq) // step_keys

    def body(j, carry):
        base = pl.multiple_of(j * step_keys, step_keys)
        for u in range(ATTN_UNROLL):
            block(base + u * ATTN_KEYS, ATTN_KEYS, False)
        return carry

    lax.fori_loop(0, n_steps, body, 0)
    left = qi - n_steps * (step_keys // tq)
    rest = pl.multiple_of(n_steps * step_keys, step_keys)
    for u in range(ATTN_UNROLL - 1):
        @pl.when(left >= (u + 1) * per)
        def _():
            block(rest + u * ATTN_KEYS, ATTN_KEYS, False)
    tail = left % per
    width, pos = per // 2, qi - tail
    while width >= 1:
        @pl.when(tail & width != 0)
        def _(width=width, pos=pos):
            block(pl.multiple_of(pos * tq, tq), width * tq, False)
        pos = pos + (tail & width)
        width //= 2

    block(pl.multiple_of(qi * tq, tq), tq, True)

    first_half = _iota((tq, LANES), 1) < HEAD_DIM
    heads_out = []
    for h in range(HEADS):
        acc = acc_ref[h * tq:(h + 1) * tq]
        heads_out.append(acc / pltpu.roll(acc, HEAD_DIM, 1))
    for g in range(HEADS // 2):
        pair = jnp.where(first_half, heads_out[2 * g], pltpu.roll(heads_out[2 * g + 1], HEAD_DIM, 1))
        o_ref[:, g * LANES:(g + 1) * LANES] = pair.astype(BF16)


def _attention(q, keys, v_ones, q_rope=None, f_rows=None):
    bsz, lp, _ = q.shape
    tq = ROW_TILE
    resident = lambda w: pl.BlockSpec((None, lp, w), lambda b, t: (b, 0, 0))
    in_specs = [_rows(tq, WIDTH)]
    args = [q]
    if q_rope is not None:
        in_specs.append(_rows(tq, HEADS * LANES))
        args.append(q_rope)
    in_specs += [resident(WIDTH)] * len(keys) + [resident(HEADS * LANES)]
    args += [*keys, v_ones]
    if f_rows is not None:
        in_specs.append(pl.BlockSpec((None, SUBLANES, lp), lambda b, t: (b, 0, 0)))
        args.append(f_rows)
    kern = functools.partial(_attn_kernel, has_rope=q_rope is not None,
                             has_bias=f_rows is not None)
    return pl.pallas_call(
        kern,
        grid=(bsz, lp // tq),
        in_specs=in_specs,
        out_specs=_rows(tq, WIDTH),
        out_shape=jax.ShapeDtypeStruct((bsz, lp, WIDTH), BF16),
        scratch_shapes=[pltpu.VMEM((HEADS * tq, LANES), F32), pltpu.VMEM((HEADS * tq, LANES), F32)],
        compiler_params=_cparams("parallel", "arbitrary"),
        name="attn_bias" if f_rows is not None else "attn_rope",
    )(*args)


def _bf(x):
    return x.astype(BF16)


def _stack_bf(x):
    return _bf(_stack_heads(x))


def _unit_lower_inverse(ms, sub_blk, eye):
    n = HEADS * RWKV_CHUNK
    eye_bd = (_iota((n, n), 0) == _iota((n, n), 1)).astype(F32)
    md = [jnp.where(sub_blk, m, 0.0) for m in ms]
    mo = [_stack_bf(m - d) for m, d in zip(ms, md)]
    x = [eye + d for d in md]
    p = md
    p_bd = [_stack_heads(t) for t in p]
    for _ in range(int(math.log2(RWKV_SUB)) - 1):
        p = [_dot(_bf(t), _bf(s)) for t, s in zip(p, p_bd)]
        p_bd = [_stack_heads(t) for t in p]
        x = [_dot(_bf(a), _bf(eye_bd + s)) for a, s in zip(x, p_bd)]
    x_bd = [_stack_bf(a) for a in x]
    pm = [_dot(_bf(a), o) for a, o in zip(x, mo)]
    pm2 = [_dot(_bf(t), _stack_bf(t)) for t in pm]
    pm3 = [_dot(_bf(a), _stack_bf(b)) for a, b in zip(pm, pm2)]
    return [_dot(_bf(eye + a + b + c), s) for a, b, c, s in zip(pm, pm2, pm3, x_bd)]


def _rwkv_kernel(h_ref, w_ref, mu_ref, w0_ref, w2_ref, a0_ref, a2_ref, g2_ref, kk_ref, ka_ref,
                 rk_ref, gng_ref, gnb_ref, ones_ref, o_ref, pcarry_ref, state_ref, y_ref):
    @pl.when(pl.program_id(0) == 0)
    def _():
        pcarry_ref[...] = jnp.zeros_like(pcarry_ref)
        state_ref[...] = jnp.zeros_like(state_ref)

    bsz, tile, _ = h_ref.shape
    ones_bd = ones_ref[...]
    wide = (RWKV_CHUNK, HEADS * RWKV_CHUNK)
    t_idx, s_idx = _iota(wide, 0), _iota(wide, 1) % RWKV_CHUNK
    strict = t_idx > s_idx
    incl = t_idx >= s_idx
    sub_blk = (t_idx // RWKV_SUB) == (s_idx // RWKV_SUB)
    eye = (t_idx == s_idx).astype(F32)
    diag = _iota((WIDTH, WIDTH), 0) == _iota((WIDTH, WIDTH), 1)
    mid = RWKV_CHUNK // 2 - 1
    chunk_rows = [slice(c * RWKV_CHUNK, (c + 1) * RWKV_CHUNK) for c in range(tile // RWKV_CHUNK)]

    seqs = []
    ar_t, b_t, k_t, a_0, r_0, b_e, k_e, v_s, w_end = ([] for _ in range(9))
    for b in range(bsz):
        p_in = _dot(h_ref[b], w_ref[...])
        prev = _shift_rows(p_in, 1, pcarry_ref[b])
        pcarry_ref[b] = p_in[tile - SUBLANES:, :]
        p = p_in + (prev - p_in) * mu_ref[...]
        r = p[:, 0:WIDTH]
        k = p[:, WIDTH:2 * WIDTH]
        v = p[:, 2 * WIDTH:3 * WIDTH]
        lora = p[:, 3 * WIDTH:]
        z = w0_ref[...] + _dot(jnp.tanh(lora).astype(BF16), w2_ref[...])
        log_w = (-RWKV_DECAY_SCALE) * _sigmoid(z)
        a_gate = _sigmoid(a0_ref[...] + _dot(lora.astype(BF16), a2_ref[...]))
        gate = _dot(_sigmoid(lora).astype(BF16), g2_ref[...])
        kk = k * kk_ref[...]
        kk = kk * lax.rsqrt(jnp.maximum(_head_sum(kk * kk, ones_bd), 1e-24))
        kf = k * (1.0 + (a_gate - 1.0) * ka_ref[...])
        a_vec = -kk
        b_vec = kk * a_gate
        seqs.append((r, kf, v, gate))
        for rows in chunk_rows:
            lw = log_w[rows]
            g_in = _cumsum_rows(lw)
            g_ex = g_in - lw
            g_mid = g_in[mid:mid + 1, :]
            g_end = g_in[RWKV_CHUNK - 1:RWKV_CHUNK, :]
            decay_out = jnp.exp(g_mid - g_in)
            to_end = jnp.exp(g_end - g_in)
            a_c, b_c, k_c, r_c = a_vec[rows], b_vec[rows], kf[rows], r[rows]
            ar_t.append(_bf(jnp.concatenate([a_c * jnp.exp(g_ex - g_mid), r_c * jnp.exp(g_in - g_mid)], axis=0)))
            b_t.append(_stack_bf(b_c * decay_out))
            k_t.append(_stack_bf(k_c * decay_out))
            a_0.append(_stack_bf(a_c * jnp.exp(g_ex)))
            r_0.append(r_c * jnp.exp(g_in))
            b_e.append(_stack_bf(b_c * to_end))
            k_e.append(_stack_bf(k_c * to_end))
            v_s.append(_stack_bf(v[rows]))
            w_end.append(jnp.exp(g_end))

    each = lambda f, *lists: [f(*args) for args in zip(*lists)]
    top, bottom = slice(0, RWKV_CHUNK), slice(RWKV_CHUNK, 2 * RWKV_CHUNK)
    gram_b = each(_dot_nt, ar_t, b_t)
    gram_k = each(_dot_nt, ar_t, k_t)
    m_ab = each(lambda g: jnp.where(strict, g[top], 0.0), gram_b)
    m_ak = each(lambda g: _bf(jnp.where(strict, g[top], 0.0)), gram_k)
    m_rb = each(lambda g: _bf(jnp.where(incl, g[bottom], 0.0)), gram_b)
    m_rk = each(lambda g: _bf(jnp.where(incl, g[bottom], 0.0)), gram_k)
    t_inv = each(_bf, _unit_lower_inverse(m_ab, sub_blk, eye))
    w_a = each(lambda t, a: _stack_bf(_dot(t, a)), t_inv, a_0)
    m_v = each(lambda m, x: _stack_bf(_dot(m, x)), m_ak, v_s)
    u_v = each(lambda t, x: _stack_bf(_dot(t, x)), t_inv, m_v)
    r_eff = each(lambda r0, m, x: r0 + _dot(m, x), r_0, m_rb, w_a)
    y_v = each(lambda mb, u, mk, x: _dot(mb, u) + _dot(mk, x), m_rb, u_v, m_rk, v_s)
    g_mat = each(lambda w, b, x: jnp.where(diag, w, 0.0) + _dot_tn(b, x), w_end, b_e, w_a)
    d_mat = each(lambda b, u, kx, x: _dot_tn(b, u) + _dot_tn(kx, x), b_e, u_v, k_e, v_s)

    for c, rows in enumerate(chunk_rows):
        for b in range(bsz):
            i = b * len(chunk_rows) + c
            state = _split(state_ref[b], 2)
            y_ref[b, rows, :] = _mdot(_dot, _split(r_eff[i], 2), state) + y_v[i]
            state_ref[b] = _mdot(_dot, _split(g_mat[i], 2), state) + d_mat[i]

    inv_n = 1.0 / HEAD_DIM
    for b, (r, kf, v, gate) in enumerate(seqs):
        y = y_ref[b]
        y_mu = _head_sum(y, ones_bd) * inv_n
        yc = y - y_mu
        y_var = _head_sum(yc * yc, ones_bd) * inv_n
        y = yc * lax.rsqrt(y_var + RWKV_GN_EPS) * gng_ref[...] + gnb_ref[...]
        y = y + _head_sum(r * kf * rk_ref[...], ones_bd) * v
        o_ref[b] = (y * gate).astype(BF16)


def _rwkv_branch(h16, w_p, mu, w0, w2, a0, a2, g2, k_k, k_a, r_k, gn_g, gn_b):
    bsz, lp, d = h16.shape
    row = lambda t: t.reshape(1, -1)
    lora_rows = lambda w, lo: jnp.zeros((RWKV_LORA, WIDTH), F32).at[lo:lo + w.shape[0]].set(w).astype(BF16)
    ones_bd = _block_diag(jnp.ones((HEADS, HEAD_DIM, HEAD_DIM), F32)).astype(BF16)
    params = [w_p.astype(BF16), row(mu), row(w0), lora_rows(w2, 0), row(a0), lora_rows(a2, 32),
              lora_rows(g2, 64), row(k_k), row(k_a), row(r_k), row(gn_g), row(gn_b), ones_bd]
    all_seqs = lambda w: pl.BlockSpec((bsz, ROW_TILE, w), lambda t: (0, t, 0))
    return pl.pallas_call(
        _rwkv_kernel,
        grid=(lp // ROW_TILE,),
        in_specs=[all_seqs(d)] + [_whole(t.shape) for t in params],
        out_specs=all_seqs(WIDTH),
        out_shape=jax.ShapeDtypeStruct((bsz, lp, WIDTH), BF16),
        scratch_shapes=[pltpu.VMEM((bsz, SUBLANES, RWKV_IN), F32), pltpu.VMEM((bsz, WIDTH, WIDTH), F32),
                        pltpu.VMEM((bsz, ROW_TILE, WIDTH), F32)],
        compiler_params=_cparams("arbitrary"),
        name="rwkv",
    )(h16, *params)


def _merge_kernel(h32_ref, h16_ref, oa_ref, ob_ref, oc_ref, od_ref, wg_ref, wb_ref, wo_ref, g_ref, b_ref,
                  o32_ref, o16_ref, *, alpha):
    h16 = h16_ref[...]
    acc = None
    for n, o_ref in enumerate((oa_ref, ob_ref, oc_ref, od_ref)):
        gate = _sigmoid(_dot(h16, wg_ref[:, n * D_MODEL:(n + 1) * D_MODEL]))
        term = gate * _dot(o_ref[...], wb_ref[n])
        acc = term if acc is None else acc + term
    mixed = _dot(acc.astype(BF16), wo_ref[...])
    y = _layer_norm(alpha * h32_ref[...] + mixed, g_ref[...], b_ref[...])
    o32_ref[...] = y
    o16_ref[...] = y.astype(BF16)


def _merge(h32, h16, branches, w_gate, w_branch, w_out, ln_g, ln_b, alpha):
    bsz, lp, d = h32.shape
    return pl.pallas_call(
        functools.partial(_merge_kernel, alpha=alpha),
        grid=(bsz, lp // ROW_TILE),
        in_specs=[_rows(ROW_TILE, d), _rows(ROW_TILE, d)] + [_rows(ROW_TILE, WIDTH)] * N_BRANCH
        + [_whole((d, N_BRANCH * d)), _whole((N_BRANCH, WIDTH, d)), _whole((d, d)), _whole((1, d)), _whole((1, d))],
        out_specs=[_rows(ROW_TILE, d), _rows(ROW_TILE, d)],
        out_shape=[jax.ShapeDtypeStruct(h32.shape, F32), jax.ShapeDtypeStruct(h32.shape, BF16)],
        compiler_params=_cparams("parallel", "parallel"),
        name="merge",
    )(h32, h16, *branches, w_gate.astype(BF16), w_branch.astype(BF16), w_out.astype(BF16),
      ln_g.reshape(1, d), ln_b.reshape(1, d))


def _ffn_kernel(h32_ref, h16_ref, wu_ref, cw_ref, cb_ref, wd_ref, g_ref, b_ref,
                o32_ref, o16_ref, carry_ref, act_ref, *, alpha):
    @pl.when(pl.program_id(1) == 0)
    def _():
        carry_ref[...] = jnp.zeros_like(carry_ref)

    h16 = h16_ref[...]

    def conv_cols(lo):
        cols = slice(lo, lo + FFN_CHUNK)
        x = _dot(h16, wu_ref[:, cols])
        carry = carry_ref[:, cols]
        out = cw_ref[FFN_CONV - 1:FFN_CONV, cols] * x + cb_ref[:, cols]
        for d in range(1, FFN_CONV):
            k = FFN_CONV - 1 - d
            out = out + cw_ref[k:k + 1, cols] * _shift_rows(x, d, carry)
        carry_ref[:, cols] = x[ROW_TILE - SUBLANES:, :]
        return out

    for j in range(D_FF // FFN_CHUNK):
        gate = conv_cols(j * FFN_CHUNK)
        val = conv_cols(D_FF + j * FFN_CHUNK)
        act_ref[:, j * FFN_CHUNK:(j + 1) * FFN_CHUNK] = (gate * _sigmoid(gate) * val).astype(BF16)
    y = _layer_norm(alpha * h32_ref[...] + _dot(act_ref[...], wd_ref[...]), g_ref[...], b_ref[...])
    o32_ref[...] = y
    o16_ref[...] = y.astype(BF16)


def _ffn(h32, h16, w_up, conv_w, conv_b, w_down, ln_g, ln_b, alpha):
    bsz, lp, d = h32.shape
    params = [w_up.astype(BF16), jnp.pad(conv_w, ((0, SUBLANES - FFN_CONV), (0, 0))), conv_b.reshape(1, -1),
              w_down.astype(BF16), ln_g.reshape(1, d), ln_b.reshape(1, d)]
    return pl.pallas_call(
        functools.partial(_ffn_kernel, alpha=alpha),
        grid=(bsz, lp // ROW_TILE),
        in_specs=[_rows(ROW_TILE, d), _rows(ROW_TILE, d)] + [_whole(t.shape) for t in params],
        out_specs=[_rows(ROW_TILE, d), _rows(ROW_TILE, d)],
        out_shape=[jax.ShapeDtypeStruct(h32.shape, F32), jax.ShapeDtypeStruct(h32.shape, BF16)],
        scratch_shapes=[pltpu.VMEM((SUBLANES, 2 * D_FF), F32), pltpu.VMEM((ROW_TILE, D_FF), BF16)],
        compiler_params=_cparams("parallel", "arbitrary"),
        name="ffn",
    )(h32, h16, *params)


def kernel(x, meta_tokens, ln_in_g, ln_in_b, w_in, w_branch, w_out, ln_mix_g, ln_mix_b, lru_conv_w, lru_conv_b, lru_w_rg, lru_b_rg, lru_w_ig, lru_b_ig, lru_lambda, fox_b_f, mla_q_norm_g, mla_kv_norm_g, mla_w_uq, mla_w_ukv, rwkv_mu, rwkv_w0, rwkv_w2, rwkv_a0, rwkv_a2, rwkv_g2, rwkv_k_k, rwkv_k_a, rwkv_r_k, rwkv_gn_g, rwkv_gn_b, ffn_w_up, ffn_conv_w, ffn_conv_b, ffn_w_down, ln_ffn_g, ln_ffn_b):
    bsz, seq, d = x.shape
    depth = w_in.shape[0]
    alpha = (2.0 * depth) ** 0.25
    t_real = N_META + seq
    lp = -(-t_real // ROW_TILE) * ROW_TILE
    meta = jnp.broadcast_to(meta_tokens.astype(x.dtype)[None], (bsz, N_META, d))
    tokens = jnp.concatenate([meta, x, jnp.zeros((bsz, lp - t_real, d), x.dtype)], axis=1)
    h32, h16 = _ln_in(tokens, ln_in_g, ln_in_b)
    cos_t, sin_t = _rotary_tables(lp)

    widths = (LRU_WIDTH, LRU_WIDTH, WIDTH, WIDTH, WIDTH, HEADS, MLA_Q_RANK, MLA_KV_RANK, MLA_ROPE,
              RWKV_IN, N_BRANCH * D_MODEL)
    offs = np.concatenate([[0], np.cumsum(widths)])
    col = lambda w, i, j=None: w[:, int(offs[i]):int(offs[(i if j is None else j) + 1])]

    for l in range(depth):
        wl = w_in[l]
        o_a = _lru_branch(h16, col(wl, 0, 1), lru_conv_w[l], lru_conv_b[l], lru_w_rg[l], lru_b_rg[l],
                          lru_w_ig[l], lru_b_ig[l], lru_lambda[l])
        w_fox = jnp.concatenate([col(wl, 2, 3), _spread_heads(col(wl, 4)), _pad_cols(col(wl, 5), LANES)], axis=1)
        fq, fk, fv, f_cum = _fox_proj(h16, w_fox, fox_b_f[l], LOG2E / math.sqrt(HEAD_DIM))
        f_rows = jnp.swapaxes(f_cum[:, :, :SUBLANES], 1, 2)
        o_b = _attention(fq, (fk,), fv, f_rows=f_rows)
        w_mla = _take_cols(wl, int(offs[6]), 4 * LANES, 4 * LANES)
        w_cq, w_ckv = w_mla[:, :MLA_Q_RANK], w_mla[:, MLA_Q_RANK:MLA_Q_RANK + MLA_KV_RANK]
        w_kr = w_mla[:, MLA_Q_RANK + MLA_KV_RANK:MLA_Q_RANK + MLA_KV_RANK + MLA_ROPE]
        w_tail = _take_cols(wl, int(offs[9]), RWKV_IN + N_BRANCH * D_MODEL, 3 * LANES)
        mq, mqr, mk01, mk23, mv = _mla_proj(h16, w_cq, w_ckv, w_kr, mla_q_norm_g[l],
                                            mla_kv_norm_g[l], mla_w_uq[l], mla_w_ukv[l], cos_t, sin_t,
                                            LOG2E / math.sqrt(MLA_NOPE + MLA_ROPE))
        o_c = _attention(mq, (mk01, mk23), mv, q_rope=mqr)
        o_d = _rwkv_branch(h16, w_tail[:, :RWKV_IN], rwkv_mu[l], rwkv_w0[l], rwkv_w2[l], rwkv_a0[l], rwkv_a2[l],
                           rwkv_g2[l], rwkv_k_k[l], rwkv_k_a[l], rwkv_r_k[l], rwkv_gn_g[l], rwkv_gn_b[l])
        h32, h16 = _merge(h32, h16, (o_a, o_b, o_c, o_d), w_tail[:, RWKV_IN:], w_branch[l], w_out[l],
                          ln_mix_g[l], ln_mix_b[l], alpha)
        h32, h16 = _ffn(h32, h16, ffn_w_up[l], ffn_conv_w[l], ffn_conv_b[l], ffn_w_down[l],
                        ln_ffn_g[l], ln_ffn_b[l], alpha)
    return h32[:, N_META:t_real]
```

```python
import functools
import math

import jax
import jax.numpy as jnp
import numpy as np
from jax import lax
from jax.experimental import pallas as pl
from jax.experimental.pallas import tpu as pltpu

F32 = jnp.float32
BF16 = jnp.bfloat16

D_MODEL = 1024
N_META = 16
LN_EPS = 1e-5
RMS_EPS = 1e-6
NEG_BIG = -1e30
LOG2E = 1.4426950408889634

LRU_WIDTH = 256
LRU_BLOCKS = 4
LRU_CONV = 4
LRU_C = 8.0
HEADS = 4
HEAD_DIM = 64
WIDTH = HEADS * HEAD_DIM
MLA_Q_RANK = 256
MLA_KV_RANK = 128
MLA_NOPE = 64
MLA_ROPE = 32
ROPE_BASE = 10000.0
RWKV_LORA = 128
RWKV_IN = 3 * WIDTH + RWKV_LORA
RWKV_GN_EPS = 64e-5
RWKV_DECAY_SCALE = math.exp(-0.5)
N_BRANCH = 4
D_FF = 2816
FFN_CONV = 3

SUBLANES = 8
LANES = 128
MXU_DIM = 256
ROW_TILE = 256
RWKV_CHUNK = 64
RWKV_SUB = 16
FFN_CHUNK = 256
ATTN_KEYS = 1024
ATTN_UNROLL = 2
VMEM_LIMIT = 56 * 1024 * 1024


def _cparams(*sem):
    return pltpu.CompilerParams(dimension_semantics=sem, vmem_limit_bytes=VMEM_LIMIT)


def _whole(shape):
    nd = len(shape)
    return pl.BlockSpec(shape, lambda *_: (0,) * nd)


def _rows(tile, width):
    return pl.BlockSpec((None, tile, width), lambda b, t: (b, t, 0))


def _iota(shape, dim):
    return lax.broadcasted_iota(jnp.int32, shape, dim)


def _dot(a, b):
    return jnp.dot(a, b, preferred_element_type=F32)


def _dot_nt(a, b):
    return lax.dot_general(a, b, (((1,), (1,)), ((), ())), preferred_element_type=F32)


def _dot_tn(a, b):
    return lax.dot_general(a, b, (((0,), (0,)), ((), ())), preferred_element_type=F32)


def _split(x, terms):
    out = []
    for _ in range(terms - 1):
        hi = x.astype(BF16)
        out.append(hi)
        x = x - hi.astype(F32)
    out.append(x.astype(BF16))
    return out


def _mdot(dot, a_terms, b_terms):
    order = max(len(a_terms), len(b_terms))
    acc = None
    for i, a in enumerate(a_terms):
        for j, b in enumerate(b_terms):
            if i + j < order:
                p = dot(a, b)
                acc = p if acc is None else acc + p
    return acc


def _sigmoid(x):
    return 1.0 / (1.0 + jnp.exp(-x))


def _softplus(x):
    return jnp.maximum(x, 0.0) + jnp.log(1.0 + jnp.exp(-jnp.abs(x)))


def _gelu_tanh(x):
    return 0.5 * x * (1.0 + jnp.tanh(math.sqrt(2.0 / math.pi) * (x + 0.044715 * (x * x * x))))


def _layer_norm(x, g, b):
    mu = jnp.mean(x, -1, keepdims=True)
    xc = x - mu
    var = jnp.mean(xc * xc, -1, keepdims=True)
    return xc * lax.rsqrt(var + LN_EPS) * g + b


def _shift_rows(x, d, carry):
    rolled = pltpu.roll(x, d, 0)
    head = rolled[:SUBLANES]
    rows = _iota(head.shape, 0)
    for r in range(d):
        src = SUBLANES - d + r
        head = jnp.where(rows == r, carry[src:src + 1, :], head)
    return jnp.concatenate([head, rolled[SUBLANES:]], axis=0)


def _cumsum_rows(x):
    rows = _iota(x.shape, 0)
    d = 1
    while d < x.shape[0]:
        x = x + jnp.where(rows >= d, pltpu.roll(x, d, 0), 0.0)
        d *= 2
    return x


def _linear_scan_rows(a, u, h_in):
    in_group = _iota(a.shape, 0) % SUBLANES
    d = 1
    while d < SUBLANES:
        keep = in_group >= d
        u = u + a * jnp.where(keep, pltpu.roll(u, d, 0), 0.0)
        a = a * jnp.where(keep, pltpu.roll(a, d, 0), 1.0)
        d *= 2
    groups = []
    for g in range(a.shape[0] // SUBLANES):
        rows = slice(g * SUBLANES, (g + 1) * SUBLANES)
        h = u[rows] + a[rows] * h_in
        h_in = h[SUBLANES - 1:SUBLANES, :]
        groups.append(h)
    return jnp.concatenate(groups, axis=0)


def _tile_lanes(x, n):
    return jnp.concatenate([x] * n, axis=1)


def _head_mask(shape, lane_dim, head):
    lane = _iota(shape, lane_dim)
    return (lane >= head * HEAD_DIM) & (lane < (head + 1) * HEAD_DIM)


def _stack_heads(x):
    return jnp.concatenate(
        [jnp.where(_head_mask(x.shape, 1, h), x, jnp.zeros_like(x)) for h in range(HEADS)], axis=0)


def _head_sum(x, ones_bd):
    return _mdot(_dot, _split(x, 2), [ones_bd])


def _ln_in_kernel(x_ref, meta_ref, g_ref, b_ref, o32_ref, o16_ref, carry_ref, *, t_real):
    t = pl.program_id(1)

    @pl.when(t == 0)
    def _():
        carry_ref[...] = meta_ref[...]

    x = x_ref[...]
    tile = x.shape[0]
    tokens = jnp.concatenate([carry_ref[...], x[:tile - N_META]], axis=0)
    carry_ref[...] = x[tile - N_META:]
    tokens = jnp.where(_iota(tokens.shape, 0) + t * tile < t_real, tokens, 0.0)
    y = _layer_norm(tokens, g_ref[...], b_ref[...])
    o32_ref[...] = y
    o16_ref[...] = y.astype(BF16)


def _ln_in(x, meta, g, b, lp):
    bsz, seq, d = x.shape
    last = seq // ROW_TILE - 1
    return pl.pallas_call(
        functools.partial(_ln_in_kernel, t_real=N_META + seq),
        grid=(bsz, lp // ROW_TILE),
        in_specs=[pl.BlockSpec((None, ROW_TILE, d), lambda b, t: (b, jnp.minimum(t, last), 0)),
                  _whole((N_META, d)), _whole((1, d)), _whole((1, d))],
        out_specs=[_rows(ROW_TILE, d), _rows(ROW_TILE, d)],
        out_shape=[jax.ShapeDtypeStruct((bsz, lp, d), F32), jax.ShapeDtypeStruct((bsz, lp, d), BF16)],
        scratch_shapes=[pltpu.VMEM((N_META, d), F32)],
        compiler_params=_cparams("parallel", "arbitrary"),
        name="ln_in",
    )(x, meta.astype(x.dtype), g.reshape(1, d), b.reshape(1, d))


def _lru_kernel(h_ref, w_ref, cw_ref, cb_ref, wrg_ref, brg_ref, wig_ref, big_ref, lam_ref,
                o_ref, xcarry_ref, hcarry_ref):
    @pl.when(pl.program_id(1) == 0)
    def _():
        xcarry_ref[...] = jnp.zeros_like(xcarry_ref)
        hcarry_ref[...] = jnp.zeros_like(hcarry_ref)

    yx = _dot(h_ref[...], w_ref[...])
    y = yx[:, :LRU_WIDTH]
    x = yx[:, LRU_WIDTH:]
    xcarry = xcarry_ref[...]
    xc = cw_ref[LRU_CONV - 1:LRU_CONV, :] * x + cb_ref[...]
    for d in range(1, LRU_CONV):
        k = LRU_CONV - 1 - d
        xc = xc + cw_ref[k:k + 1, :] * _shift_rows(x, d, xcarry)
    xcarry_ref[...] = x[ROW_TILE - SUBLANES:, :]

    xc16 = xc.astype(BF16)
    r = _sigmoid(_dot(xc16, wrg_ref[...]) + brg_ref[...])
    i = _sigmoid(_dot(xc16, wig_ref[...]) + big_ref[...])
    log_a = (-LRU_C) * r * _softplus(-lam_ref[...])
    a = jnp.exp(log_a)
    th = jnp.tanh(log_a)
    u = jnp.sqrt(-2.0 * th / (1.0 - th)) * (i * xc)
    h = _linear_scan_rows(a, u, hcarry_ref[SUBLANES - 1:SUBLANES, :])
    hcarry_ref[...] = h[ROW_TILE - SUBLANES:, :]
    o_ref[...] = (_gelu_tanh(y) * h).astype(BF16)


def _block_diag(w):
    n, d, e = w.shape
    eye = jnp.eye(n, dtype=w.dtype)
    return (eye[:, None, :, None] * w[:, :, None, :]).reshape(n * d, n * e)


def _lru_branch(h16, w_yx, conv_w, conv_b, w_rg, b_rg, w_ig, b_ig, lam):
    bsz, lp, d = h16.shape
    row = lambda v: v.reshape(1, LRU_WIDTH)
    return pl.pallas_call(
        _lru_kernel,
        grid=(bsz, lp // ROW_TILE),
        in_specs=[_rows(ROW_TILE, d), _whole((d, 2 * LRU_WIDTH)),
                  _whole((LRU_CONV, LRU_WIDTH)), _whole((1, LRU_WIDTH)),
                  _whole((LRU_WIDTH, LRU_WIDTH)), _whole((1, LRU_WIDTH)),
                  _whole((LRU_WIDTH, LRU_WIDTH)), _whole((1, LRU_WIDTH)),
                  _whole((1, LRU_WIDTH))],
        out_specs=_rows(ROW_TILE, LRU_WIDTH),
        out_shape=jax.ShapeDtypeStruct((bsz, lp, LRU_WIDTH), BF16),
        scratch_shapes=[pltpu.VMEM((SUBLANES, LRU_WIDTH), F32), pltpu.VMEM((SUBLANES, LRU_WIDTH), F32)],
        compiler_params=_cparams("parallel", "arbitrary"),
        name="lru",
    )(h16, w_yx.astype(BF16), conv_w, row(conv_b), _block_diag(w_rg).astype(BF16), row(b_rg),
      _block_diag(w_ig).astype(BF16), row(b_ig), row(lam))


def _values_with_ones(v_spread):
    lane = _iota(v_spread.shape, 1)
    return jnp.where(lane % LANES >= HEAD_DIM, 1.0, v_spread).astype(BF16)


def _spread_heads(w):
    k = w.shape[0]
    return jnp.pad(w.reshape(k, HEADS, HEAD_DIM), ((0, 0), (0, 0), (0, LANES - HEAD_DIM))).reshape(k, HEADS * LANES)


def _fox_proj_kernel(h_ref, w_ref, bf_ref, q_ref, k_ref, v_ref, f_ref, fcarry_ref, *, q_scale):
    @pl.when(pl.program_id(1) == 0)
    def _():
        fcarry_ref[...] = jnp.zeros_like(fcarry_ref)

    z = _dot(h_ref[...], w_ref[...])
    q_ref[...] = (z[:, 0:WIDTH] * q_scale).astype(BF16)
    k_ref[...] = z[:, WIDTH:2 * WIDTH].astype(BF16)
    v_ref[...] = _values_with_ones(z[:, 2 * WIDTH:2 * WIDTH + HEADS * LANES])
    log_f = -_softplus(-(z[:, 2 * WIDTH + HEADS * LANES:] + bf_ref[...]))
    f = _cumsum_rows(log_f) + fcarry_ref[SUBLANES - 1:SUBLANES, :]
    fcarry_ref[...] = f[ROW_TILE - SUBLANES:, :]
    f_ref[...] = f


def _fox_proj(h16, w_qkvf, b_f, q_scale):
    bsz, lp, d = h16.shape
    bf = jnp.zeros((1, LANES), F32).at[0, :HEADS].set(b_f)
    act = lambda w: jax.ShapeDtypeStruct((bsz, lp, w), BF16)
    return pl.pallas_call(
        functools.partial(_fox_proj_kernel, q_scale=q_scale),
        grid=(bsz, lp // ROW_TILE),
        in_specs=[_rows(ROW_TILE, d), _whole(w_qkvf.shape), _whole((1, LANES))],
        out_specs=[_rows(ROW_TILE, WIDTH), _rows(ROW_TILE, WIDTH), _rows(ROW_TILE, HEADS * LANES),
                   _rows(ROW_TILE, LANES)],
        out_shape=[act(WIDTH), act(WIDTH), act(HEADS * LANES), jax.ShapeDtypeStruct((bsz, lp, LANES), F32)],
        scratch_shapes=[pltpu.VMEM((SUBLANES, LANES), F32)],
        compiler_params=_cparams("parallel", "arbitrary"),
        name="fox_proj",
    )(h16, w_qkvf.astype(BF16), bf)


def _rms_norm(x, g):
    return x * lax.rsqrt(jnp.mean(x * x, -1, keepdims=True) + RMS_EPS) * g


def _mla_proj_kernel(h_ref, win_ref, qg_ref, kvg_ref, wq_ref, wkv_ref, cos_ref, sin_ref,
                     qn_ref, qr_ref, k01_ref, k23_ref, v_ref, *, q_scale):
    z = _dot(h_ref[...], win_ref[...])
    c_q = z[:, 0:MLA_Q_RANK]
    c_kv = z[:, MLA_Q_RANK:MLA_Q_RANK + MLA_KV_RANK]
    k_r = z[:, MLA_Q_RANK + MLA_KV_RANK:MLA_Q_RANK + MLA_KV_RANK + LANES]
    k_r_swapped = z[:, MLA_Q_RANK + MLA_KV_RANK + LANES:]
    cos = cos_ref[...]
    sin = sin_ref[...]

    q = _dot(_rms_norm(c_q, qg_ref[...]).astype(BF16), wq_ref[...])
    qn_ref[...] = (q[:, 0:WIDTH] * q_scale).astype(BF16)
    rope_w = HEADS * LANES
    for h in range(HEADS):
        plain = q[:, WIDTH + h * LANES:WIDTH + (h + 1) * LANES]
        swapped = q[:, WIDTH + rope_w + h * LANES:WIDTH + rope_w + (h + 1) * LANES]
        qr_ref[:, h * LANES:(h + 1) * LANES] = ((plain * cos + swapped * sin) * q_scale).astype(BF16)

    kv = _dot(_rms_norm(c_kv, kvg_ref[...]).astype(BF16), wkv_ref[...])
    k_rope = (k_r * cos + k_r_swapped * sin).astype(BF16)
    for g, k_ref in enumerate((k01_ref, k23_ref)):
        k_ref[:, 0:LANES] = kv[:, g * LANES:(g + 1) * LANES].astype(BF16)
        k_ref[:, LANES:] = k_rope
    v_ref[...] = _values_with_ones(kv[:, WIDTH:])


def _swap_halves(w):
    half = w.shape[-1] // 2
    return jnp.concatenate([w[..., half:], w[..., :half]], axis=-1)


def _pad_cols(w, width):
    return jnp.pad(w, ((0, 0), (0, width - w.shape[1])))


def _take_cols_kernel(a_ref, b_ref, o_ref, *, shift):
    both = jnp.concatenate([a_ref[...], b_ref[...]], axis=1)
    o_ref[...] = both[:, shift:shift + o_ref.shape[1]].astype(BF16)


def _take_cols(w, layer, start, width, block):
    rows = w.shape[1]
    first, shift = divmod(start, block)
    assert (first + width // block) * block < w.shape[2]
    return pl.pallas_call(
        functools.partial(_take_cols_kernel, shift=shift),
        grid=(width // block,),
        in_specs=[pl.BlockSpec((None, rows, block), lambda j: (layer, 0, j + first)),
                  pl.BlockSpec((None, rows, block), lambda j: (layer, 0, j + first + 1))],
        out_specs=pl.BlockSpec((rows, block), lambda j: (0, j)),
        out_shape=jax.ShapeDtypeStruct((rows, width), BF16),
        compiler_params=_cparams("parallel"),
        name="take_cols",
    )(w, w)


def _mla_proj(h16, w_cq, w_ckv, w_kr, q_norm_g, kv_norm_g, w_uq, w_ukv, cos_t, sin_t, q_scale):
    bsz, lp, d = h16.shape
    w_in = jnp.concatenate([w_cq, w_ckv, _pad_cols(w_kr, LANES), _pad_cols(_swap_halves(w_kr), LANES)], axis=1)
    uq = w_uq.reshape(MLA_Q_RANK, HEADS, MLA_NOPE + MLA_ROPE)
    q_nope = uq[:, :, :MLA_NOPE].reshape(MLA_Q_RANK, WIDTH)
    q_rope = uq[:, :, MLA_NOPE:]
    lane_pad = lambda r: jnp.pad(r, ((0, 0), (0, 0), (0, LANES - MLA_ROPE))).reshape(MLA_Q_RANK, HEADS * LANES)
    wq = jnp.concatenate([q_nope, lane_pad(q_rope), lane_pad(_swap_halves(q_rope))], axis=1)
    ukv = w_ukv.reshape(MLA_KV_RANK, HEADS, MLA_NOPE + HEAD_DIM)
    wkv = jnp.concatenate([ukv[:, :, :MLA_NOPE].reshape(MLA_KV_RANK, WIDTH),
                           _spread_heads(ukv[:, :, MLA_NOPE:].reshape(MLA_KV_RANK, WIDTH))], axis=1)
    table = pl.BlockSpec((ROW_TILE, LANES), lambda b, t: (t, 0))
    act = lambda w: jax.ShapeDtypeStruct((bsz, lp, w), BF16)
    return pl.pallas_call(
        functools.partial(_mla_proj_kernel, q_scale=q_scale),
        grid=(bsz, lp // ROW_TILE),
        in_specs=[_rows(ROW_TILE, d), _whole(w_in.shape), _whole((1, MLA_Q_RANK)), _whole((1, MLA_KV_RANK)),
                  _whole(wq.shape), _whole(wkv.shape), table, table],
        out_specs=[_rows(ROW_TILE, WIDTH), _rows(ROW_TILE, HEADS * LANES),
                   _rows(ROW_TILE, WIDTH), _rows(ROW_TILE, WIDTH), _rows(ROW_TILE, HEADS * LANES)],
        out_shape=[act(WIDTH), act(HEADS * LANES), act(WIDTH), act(WIDTH), act(HEADS * LANES)],
        compiler_params=_cparams("parallel", "parallel"),
        name="mla_proj",
    )(h16, w_in.astype(BF16), q_norm_g.reshape(1, -1), kv_norm_g.reshape(1, -1),
      wq.astype(BF16), wkv.astype(BF16), cos_t, sin_t)


def _rotary_tables(lp):
    inv = 1.0 / (ROPE_BASE ** (jnp.arange(0, MLA_ROPE, 2, dtype=F32) / MLA_ROPE))
    ang = jnp.arange(lp, dtype=F32)[:, None] * inv[None, :]
    cos, sin = jnp.cos(ang), jnp.sin(ang)
    pad = jnp.zeros((lp, LANES - MLA_ROPE), F32)
    return (jnp.concatenate([cos, cos, pad], axis=1), jnp.concatenate([-sin, sin, pad], axis=1))


def _attn_kernel(*refs, has_rope, has_bias):
    it = iter(refs)
    q_ref = next(it)
    qr_ref = next(it) if has_rope else None
    k_refs = [next(it) for _ in range(2 if has_rope else 1)]
    v_ref = next(it)
    f_ref = next(it) if has_bias else None
    o_ref = next(it)
    m_ref, acc_ref = next(it), next(it)

    tq = q_ref.shape[0]
    qi = pl.program_id(1)
    heads_per_group = HEADS // len(k_refs)
    q_stacks = []
    for g in range(len(k_refs)):
        parts = []
        for h in range(g * heads_per_group, (g + 1) * heads_per_group):
            if has_rope:
                qg = q_ref[:, g * LANES:(g + 1) * LANES]
                qh = jnp.where(_head_mask(qg.shape, 1, h % heads_per_group), qg, jnp.zeros_like(qg))
                qh = jnp.concatenate([qh, qr_ref[:, h * LANES:(h + 1) * LANES]], axis=1)
            else:
                q = q_ref[...]
                qh = jnp.where(_head_mask(q.shape, 1, h), q, jnp.zeros_like(q))
            parts.append(qh)
        q_stacks.append(jnp.concatenate(parts, axis=0))

    m_ref[...] = jnp.full(m_ref.shape, NEG_BIG, F32)
    acc_ref[...] = jnp.zeros_like(acc_ref)
    if has_bias:
        f_q0 = f_ref[:, pl.ds(pl.multiple_of(qi * tq, tq), LANES)][:, 0:1]

    def block(start, tk, masked):
        scores = [_dot_nt(qs, k_ref[pl.ds(start, tk), :]) for qs, k_ref in zip(q_stacks, k_refs)]
        if has_bias:
            bias = (f_q0 - f_ref[:, pl.ds(start, tk)]) * LOG2E
        if masked:
            keep = _iota((tq, tk), 0) >= _iota((tq, tk), 1)
        for h in range(HEADS):
            rows = slice(h * tq, (h + 1) * tq)
            in_group = h % heads_per_group
            sh = scores[h // heads_per_group][in_group * tq:(in_group + 1) * tq]
            if has_bias:
                sh = sh + bias[h:h + 1, :]
            if masked:
                sh = jnp.where(keep, sh, NEG_BIG)
            m_old = m_ref[rows]
            m_new = jnp.maximum(m_old, jnp.max(sh, axis=1, keepdims=True))
            p = jnp.exp2(sh - _tile_lanes(m_new, tk // LANES))
            alpha = jnp.exp2(m_old - m_new)
            m_ref[rows] = m_new
            v_ones = v_ref[pl.ds(start, tk), h * LANES:(h + 1) * LANES]
            acc_ref[rows] = acc_ref[rows] * alpha + _dot(p.astype(BF16), v_ones)

    per = ATTN_KEYS // tq
    step_keys = ATTN_UNROLL * ATTN_KEYS
    n_steps = (qi * tq) // step_keys

    def body(j, carry):
        base = pl.multiple_of(j * step_keys, step_keys)
        for u in range(ATTN_UNROLL):
            block(base + u * ATTN_KEYS, ATTN_KEYS, False)
        return carry

    lax.fori_loop(0, n_steps, body, 0)
    left = qi - n_steps * (step_keys // tq)
    rest = pl.multiple_of(n_steps * step_keys, step_keys)
    for u in range(ATTN_UNROLL - 1):
        @pl.when(left >= (u + 1) * per)
        def _():
            block(rest + u * ATTN_KEYS, ATTN_KEYS, False)
    tail = left % per
    width, pos = per // 2, qi - tail
    while width >= 1:
        @pl.when(tail & width != 0)
        def _(width=width, pos=pos):
            block(pl.multiple_of(pos * tq, tq), width * tq, False)
        pos = pos + (tail & width)
        width //= 2

    block(pl.multiple_of(qi * tq, tq), tq, True)

    first_half = _iota((tq, LANES), 1) < HEAD_DIM
    heads_out = []
    for h in range(HEADS):
        acc = acc_ref[h * tq:(h + 1) * tq]
        heads_out.append(acc / pltpu.roll(acc, HEAD_DIM, 1))
    for g in range(HEADS // 2):
        pair = jnp.where(first_half, heads_out[2 * g], pltpu.roll(heads_out[2 * g + 1], HEAD_DIM, 1))
        o_ref[:, g * LANES:(g + 1) * LANES] = pair.astype(BF16)


def _attention(q, keys, v_ones, q_rope=None, f_rows=None):
    bsz, lp, _ = q.shape
    tq = ROW_TILE
    resident = lambda w: pl.BlockSpec((None, lp, w), lambda b, t: (b, 0, 0))
    in_specs = [_rows(tq, WIDTH)]
    args = [q]
    if q_rope is not None:
        in_specs.append(_rows(tq, HEADS * LANES))
        args.append(q_rope)
    in_specs += [resident(WIDTH)] * len(keys) + [resident(HEADS * LANES)]
    args += [*keys, v_ones]
    if f_rows is not None:
        in_specs.append(pl.BlockSpec((None, SUBLANES, lp), lambda b, t: (b, 0, 0)))
        args.append(f_rows)
    kern = functools.partial(_attn_kernel, has_rope=q_rope is not None,
                             has_bias=f_rows is not None)
    return pl.pallas_call(
        kern,
        grid=(bsz, lp // tq),
        in_specs=in_specs,
        out_specs=_rows(tq, WIDTH),
        out_shape=jax.ShapeDtypeStruct((bsz, lp, WIDTH), BF16),
        scratch_shapes=[pltpu.VMEM((HEADS * tq, LANES), F32), pltpu.VMEM((HEADS * tq, LANES), F32)],
        compiler_params=_cparams("parallel", "arbitrary"),
        name="attn_bias" if f_rows is not None else "attn_rope",
    )(*args)


def _bf(x):
    return x.astype(BF16)


def _stack_bf(x):
    return _bf(_stack_heads(x))


def _unit_lower_inverse(ms, sub_blk, eye):
    n = HEADS * RWKV_CHUNK
    eye_bd = (_iota((n, n), 0) == _iota((n, n), 1)).astype(F32)
    md = [jnp.where(sub_blk, m, 0.0) for m in ms]
    mo = [_stack_bf(m - d) for m, d in zip(ms, md)]
    x = [eye + d for d in md]
    p = md
    p_bd = [_stack_heads(t) for t in p]
    for _ in range(int(math.log2(RWKV_SUB)) - 1):
        p = [_dot(_bf(t), _bf(s)) for t, s in zip(p, p_bd)]
        p_bd = [_stack_heads(t) for t in p]
        x = [_dot(_bf(a), _bf(eye_bd + s)) for a, s in zip(x, p_bd)]
    x_bd = [_stack_bf(a) for a in x]
    pm = [_dot(_bf(a), o) for a, o in zip(x, mo)]
    pm2 = [_dot(_bf(t), _stack_bf(t)) for t in pm]
    pm3 = [_dot(_bf(a), _stack_bf(b)) for a, b in zip(pm, pm2)]
    return [_dot(_bf(eye + a + b + c), s) for a, b, c, s in zip(pm, pm2, pm3, x_bd)]


def _rwkv_kernel(h_ref, w_ref, mu_ref, w0_ref, w2_ref, a0_ref, a2_ref, g2_ref, kk_ref, ka_ref,
                 rk_ref, gng_ref, gnb_ref, ones_ref, o_ref, pcarry_ref, state_ref, y_ref):
    @pl.when(pl.program_id(0) == 0)
    def _():
        pcarry_ref[...] = jnp.zeros_like(pcarry_ref)
        state_ref[...] = jnp.zeros_like(state_ref)

    bsz, tile, _ = h_ref.shape
    ones_bd = ones_ref[...]
    wide = (RWKV_CHUNK, HEADS * RWKV_CHUNK)
    t_idx, s_idx = _iota(wide, 0), _iota(wide, 1) % RWKV_CHUNK
    strict = t_idx > s_idx
    incl = t_idx >= s_idx
    sub_blk = (t_idx // RWKV_SUB) == (s_idx // RWKV_SUB)
    eye = (t_idx == s_idx).astype(F32)
    diag = _iota((WIDTH, WIDTH), 0) == _iota((WIDTH, WIDTH), 1)
    mid = RWKV_CHUNK // 2 - 1
    chunk_rows = [slice(c * RWKV_CHUNK, (c + 1) * RWKV_CHUNK) for c in range(tile // RWKV_CHUNK)]

    seqs = []
    ar_t, b_t, k_t, a_0, r_0, b_e, k_e, v_s, w_end = ([] for _ in range(9))
    for b in range(bsz):
        p_in = _dot(h_ref[b], w_ref[...])
        prev = _shift_rows(p_in, 1, pcarry_ref[b])
        pcarry_ref[b] = p_in[tile - SUBLANES:, :]
        p = p_in + (prev - p_in) * mu_ref[...]
        r = p[:, 0:WIDTH]
        k = p[:, WIDTH:2 * WIDTH]
        v = p[:, 2 * WIDTH:3 * WIDTH]
        lora = p[:, 3 * WIDTH:]
        z = w0_ref[...] + _dot(jnp.tanh(lora).astype(BF16), w2_ref[...])
        log_w = (-RWKV_DECAY_SCALE) * _sigmoid(z)
        a_gate = _sigmoid(a0_ref[...] + _dot(lora.astype(BF16), a2_ref[...]))
        gate = _dot(_sigmoid(lora).astype(BF16), g2_ref[...])
        kk = k * kk_ref[...]
        kk = kk * lax.rsqrt(jnp.maximum(_head_sum(kk * kk, ones_bd), 1e-24))
        kf = k * (1.0 + (a_gate - 1.0) * ka_ref[...])
        a_vec = -kk
        b_vec = kk * a_gate
        seqs.append((r, kf, v, gate))
        for rows in chunk_rows:
            lw = log_w[rows]
            g_in = _cumsum_rows(lw)
            g_ex = g_in - lw
            g_mid = g_in[mid:mid + 1, :]
            g_end = g_in[RWKV_CHUNK - 1:RWKV_CHUNK, :]
            decay_out = jnp.exp(g_mid - g_in)
            to_end = jnp.exp(g_end - g_in)
            a_c, b_c, k_c, r_c = a_vec[rows], b_vec[rows], kf[rows], r[rows]
            ar_t.append(_bf(jnp.concatenate([a_c * jnp.exp(g_ex - g_mid), r_c * jnp.exp(g_in - g_mid)], axis=0)))
            b_t.append(_stack_bf(b_c * decay_out))
            k_t.append(_stack_bf(k_c * decay_out))
            a_0.append(_stack_bf(a_c * jnp.exp(g_ex)))
            r_0.append(r_c * jnp.exp(g_in))
            b_e.append(_stack_bf(b_c * to_end))
            k_e.append(_stack_bf(k_c * to_end))
            v_s.append(_stack_bf(v[rows]))
            w_end.append(jnp.exp(g_end))

    each = lambda f, *lists: [f(*args) for args in zip(*lists)]
    top, bottom = slice(0, RWKV_CHUNK), slice(RWKV_CHUNK, 2 * RWKV_CHUNK)
    gram_b = each(_dot_nt, ar_t, b_t)
    gram_k = each(_dot_nt, ar_t, k_t)
    m_ab = each(lambda g: jnp.where(strict, g[top], 0.0), gram_b)
    m_ak = each(lambda g: _bf(jnp.where(strict, g[top], 0.0)), gram_k)
    m_rb = each(lambda g: _bf(jnp.where(incl, g[bottom], 0.0)), gram_b)
    m_rk = each(lambda g: _bf(jnp.where(incl, g[bottom], 0.0)), gram_k)
    t_inv = each(_bf, _unit_lower_inverse(m_ab, sub_blk, eye))
    w_a = each(lambda t, a: _stack_bf(_dot(t, a)), t_inv, a_0)
    m_v = each(lambda m, x: _stack_bf(_dot(m, x)), m_ak, v_s)
    u_v = each(lambda t, x: _stack_bf(_dot(t, x)), t_inv, m_v)
    r_eff = each(lambda r0, m, x: r0 + _dot(m, x), r_0, m_rb, w_a)
    y_v = each(lambda mb, u, mk, x: _dot(mb, u) + _dot(mk, x), m_rb, u_v, m_rk, v_s)
    g_mat = each(lambda w, b, x: jnp.where(diag, w, 0.0) + _dot_tn(b, x), w_end, b_e, w_a)
    d_mat = each(lambda b, u, kx, x: _dot_tn(b, u) + _dot_tn(kx, x), b_e, u_v, k_e, v_s)

    for c, rows in enumerate(chunk_rows):
        for b in range(bsz):
            i = b * len(chunk_rows) + c
            state = _split(state_ref[b], 2)
            y_ref[b, rows, :] = _mdot(_dot, _split(r_eff[i], 2), state) + y_v[i]
            state_ref[b] = _mdot(_dot, _split(g_mat[i], 2), state) + d_mat[i]

    inv_n = 1.0 / HEAD_DIM
    for b, (r, kf, v, gate) in enumerate(seqs):
        y = y_ref[b]
        y_mu = _head_sum(y, ones_bd) * inv_n
        yc = y - y_mu
        y_var = _head_sum(yc * yc, ones_bd) * inv_n
        y = yc * lax.rsqrt(y_var + RWKV_GN_EPS) * gng_ref[...] + gnb_ref[...]
        y = y + _head_sum(r * kf * rk_ref[...], ones_bd) * v
        o_ref[b] = (y * gate).astype(BF16)


def _rwkv_branch(h16, w_p, mu, w0, w2, a0, a2, g2, k_k, k_a, r_k, gn_g, gn_b):
    bsz, lp, d = h16.shape
    row = lambda t: t.reshape(1, -1)
    lora_rows = lambda w, lo: jnp.zeros((RWKV_LORA, WIDTH), F32).at[lo:lo + w.shape[0]].set(w).astype(BF16)
    ones_bd = _block_diag(jnp.ones((HEADS, HEAD_DIM, HEAD_DIM), F32)).astype(BF16)
    params = [w_p.astype(BF16), row(mu), row(w0), lora_rows(w2, 0), row(a0), lora_rows(a2, 32),
              lora_rows(g2, 64), row(k_k), row(k_a), row(r_k), row(gn_g), row(gn_b), ones_bd]
    all_seqs = lambda w: pl.BlockSpec((bsz, ROW_TILE, w), lambda t: (0, t, 0))
    return pl.pallas_call(
        _rwkv_kernel,
        grid=(lp // ROW_TILE,),
        in_specs=[all_seqs(d)] + [_whole(t.shape) for t in params],
        out_specs=all_seqs(WIDTH),
        out_shape=jax.ShapeDtypeStruct((bsz, lp, WIDTH), BF16),
        scratch_shapes=[pltpu.VMEM((bsz, SUBLANES, RWKV_IN), F32), pltpu.VMEM((bsz, WIDTH, WIDTH), F32),
                        pltpu.VMEM((bsz, ROW_TILE, WIDTH), F32)],
        compiler_params=_cparams("arbitrary"),
        name="rwkv",
    )(h16, *params)


def _merge_kernel(h32_ref, h16_ref, oa_ref, ob_ref, oc_ref, od_ref, wg_ref, wb_ref, wo_ref, g_ref, b_ref,
                  o32_ref, o16_ref, *, alpha):
    h16 = h16_ref[...]
    acc = None
    for n, o_ref in enumerate((oa_ref, ob_ref, oc_ref, od_ref)):
        gate = _sigmoid(_dot(h16, wg_ref[:, n * D_MODEL:(n + 1) * D_MODEL]))
        term = gate * _dot(o_ref[...], wb_ref[n])
        acc = term if acc is None else acc + term
    mixed = _dot(acc.astype(BF16), wo_ref[...])
    y = _layer_norm(alpha * h32_ref[...] + mixed, g_ref[...], b_ref[...])
    o32_ref[...] = y
    o16_ref[...] = y.astype(BF16)


def _merge(h32, h16, branches, w_gate, w_branch, w_out, ln_g, ln_b, alpha):
    bsz, lp, d = h32.shape
    return pl.pallas_call(
        functools.partial(_merge_kernel, alpha=alpha),
        grid=(bsz, lp // ROW_TILE),
        in_specs=[_rows(ROW_TILE, d), _rows(ROW_TILE, d)] + [_rows(ROW_TILE, WIDTH)] * N_BRANCH
        + [_whole((d, N_BRANCH * d)), _whole((N_BRANCH, WIDTH, d)), _whole((d, d)), _whole((1, d)), _whole((1, d))],
        out_specs=[_rows(ROW_TILE, d), _rows(ROW_TILE, d)],
        out_shape=[jax.ShapeDtypeStruct(h32.shape, F32), jax.ShapeDtypeStruct(h32.shape, BF16)],
        compiler_params=_cparams("parallel", "parallel"),
        name="merge",
    )(h32, h16, *branches, w_gate.astype(BF16), w_branch.astype(BF16), w_out.astype(BF16),
      ln_g.reshape(1, d), ln_b.reshape(1, d))


def _ffn_kernel(h32_ref, h16_ref, wu_ref, cw_ref, cb_ref, wd_ref, g_ref, b_ref, *rest, alpha, last_layer):
    if last_layer:
        out_ref, carry_ref, act_ref, prev_ref = rest
    else:
        o32_ref, o16_ref, carry_ref, act_ref = rest

    @pl.when(pl.program_id(1) == 0)
    def _():
        carry_ref[...] = jnp.zeros_like(carry_ref)
        if last_layer:
            prev_ref[...] = jnp.zeros_like(prev_ref)

    h16 = h16_ref[...]

    def conv_cols(lo):
        cols = slice(lo, lo + FFN_CHUNK)
        x = _dot(h16, wu_ref[:, cols])
        carry = carry_ref[:, cols]
        out = cw_ref[FFN_CONV - 1:FFN_CONV, cols] * x + cb_ref[:, cols]
        for d in range(1, FFN_CONV):
            k = FFN_CONV - 1 - d
            out = out + cw_ref[k:k + 1, cols] * _shift_rows(x, d, carry)
        carry_ref[:, cols] = x[ROW_TILE - SUBLANES:, :]
        return out

    for j in range(D_FF // FFN_CHUNK):
        gate = conv_cols(j * FFN_CHUNK)
        val = conv_cols(D_FF + j * FFN_CHUNK)
        act_ref[:, j * FFN_CHUNK:(j + 1) * FFN_CHUNK] = (gate * _sigmoid(gate) * val).astype(BF16)
    y = _layer_norm(alpha * h32_ref[...] + _dot(act_ref[...], wd_ref[...]), g_ref[...], b_ref[...])
    if last_layer:
        out_ref[...] = jnp.concatenate([prev_ref[N_META:, :], y[:N_META]], axis=0)
        prev_ref[...] = y
    else:
        o32_ref[...] = y
        o16_ref[...] = y.astype(BF16)


def _ffn(h32, h16, w_up, conv_w, conv_b, w_down, ln_g, ln_b, alpha, last_layer):
    bsz, lp, d = h32.shape
    params = [w_up.astype(BF16), jnp.pad(conv_w, ((0, SUBLANES - FFN_CONV), (0, 0))), conv_b.reshape(1, -1),
              w_down.astype(BF16), ln_g.reshape(1, d), ln_b.reshape(1, d)]
    scratch = [pltpu.VMEM((SUBLANES, 2 * D_FF), F32), pltpu.VMEM((ROW_TILE, D_FF), BF16)]
    if last_layer:
        out_specs = pl.BlockSpec((None, ROW_TILE, d), lambda b, t: (b, jnp.maximum(t - 1, 0), 0))
        out_shape = jax.ShapeDtypeStruct((bsz, lp - ROW_TILE, d), F32)
        scratch.append(pltpu.VMEM((ROW_TILE, d), F32))
    else:
        out_specs = [_rows(ROW_TILE, d), _rows(ROW_TILE, d)]
        out_shape = [jax.ShapeDtypeStruct(h32.shape, F32), jax.ShapeDtypeStruct(h32.shape, BF16)]
    return pl.pallas_call(
        functools.partial(_ffn_kernel, alpha=alpha, last_layer=last_layer),
        grid=(bsz, lp // ROW_TILE),
        in_specs=[_rows(ROW_TILE, d), _rows(ROW_TILE, d)] + [_whole(t.shape) for t in params],
        out_specs=out_specs,
        out_shape=out_shape,
        scratch_shapes=scratch,
        compiler_params=_cparams("parallel", "arbitrary"),
        name="ffn_last" if last_layer else "ffn",
    )(h32, h16, *params)


def kernel(x, meta_tokens, ln_in_g, ln_in_b, w_in, w_branch, w_out, ln_mix_g, ln_mix_b, lru_conv_w, lru_conv_b, lru_w_rg, lru_b_rg, lru_w_ig, lru_b_ig, lru_lambda, fox_b_f, mla_q_norm_g, mla_kv_norm_g, mla_w_uq, mla_w_ukv, rwkv_mu, rwkv_w0, rwkv_w2, rwkv_a0, rwkv_a2, rwkv_g2, rwkv_k_k, rwkv_k_a, rwkv_r_k, rwkv_gn_g, rwkv_gn_b, ffn_w_up, ffn_conv_w, ffn_conv_b, ffn_w_down, ln_ffn_g, ln_ffn_b):
    bsz, seq, d = x.shape
    depth = w_in.shape[0]
    alpha = (2.0 * depth) ** 0.25
    assert seq % ROW_TILE == 0 and meta_tokens.shape == (N_META, d)
    lp = seq + ROW_TILE
    h32, h16 = _ln_in(x, meta_tokens, ln_in_g, ln_in_b, lp)
    cos_t, sin_t = _rotary_tables(lp)

    widths = (LRU_WIDTH, LRU_WIDTH, WIDTH, WIDTH, WIDTH, HEADS, MLA_Q_RANK, MLA_KV_RANK, MLA_ROPE,
              RWKV_IN, N_BRANCH * D_MODEL)
    offs = np.concatenate([[0], np.cumsum(widths)])
    col = lambda l, i, j=None: w_in[l, :, int(offs[i]):int(offs[(i if j is None else j) + 1])]

    for l in range(depth):
        o_a = _lru_branch(h16, col(l,0, 1), lru_conv_w[l], lru_conv_b[l], lru_w_rg[l], lru_b_rg[l],
                          lru_w_ig[l], lru_b_ig[l], lru_lambda[l])
        w_fox = jnp.concatenate([col(l,2, 3), _spread_heads(col(l,4)), _pad_cols(col(l,5), LANES)], axis=1)
        fq, fk, fv, f_cum = _fox_proj(h16, w_fox, fox_b_f[l], LOG2E / math.sqrt(HEAD_DIM))
        f_rows = jnp.swapaxes(f_cum[:, :, :SUBLANES], 1, 2)
        o_b = _attention(fq, (fk,), fv, f_rows=f_rows)
        w_mla = _take_cols(w_in, l,int(offs[6]), 4 * LANES, 4 * LANES)
        w_cq, w_ckv = w_mla[:, :MLA_Q_RANK], w_mla[:, MLA_Q_RANK:MLA_Q_RANK + MLA_KV_RANK]
        w_kr = w_mla[:, MLA_Q_RANK + MLA_KV_RANK:MLA_Q_RANK + MLA_KV_RANK + MLA_ROPE]
        w_tail = _take_cols(w_in, l,int(offs[9]), RWKV_IN + N_BRANCH * D_MODEL, 3 * LANES)
        mq, mqr, mk01, mk23, mv = _mla_proj(h16, w_cq, w_ckv, w_kr, mla_q_norm_g[l],
                                            mla_kv_norm_g[l], mla_w_uq[l], mla_w_ukv[l], cos_t, sin_t,
                                            LOG2E / math.sqrt(MLA_NOPE + MLA_ROPE))
        o_c = _attention(mq, (mk01, mk23), mv, q_rope=mqr)
        o_d = _rwkv_branch(h16, w_tail[:, :RWKV_IN], rwkv_mu[l], rwkv_w0[l], rwkv_w2[l], rwkv_a0[l], rwkv_a2[l],
                           rwkv_g2[l], rwkv_k_k[l], rwkv_k_a[l], rwkv_r_k[l], rwkv_gn_g[l], rwkv_gn_b[l])
        h32, h16 = _merge(h32, h16, (o_a, o_b, o_c, o_d), w_tail[:, RWKV_IN:], w_branch[l], w_out[l],
                          ln_mix_g[l], ln_mix_b[l], alpha)
        out = _ffn(h32, h16, ffn_w_up[l], ffn_conv_w[l], ffn_conv_b[l], ffn_w_down[l],
                   ln_ffn_g[l], ln_ffn_b[l], alpha, last_layer=l == depth - 1)
        if l < depth - 1:
            h32, h16 = out
    return out
```

```python
import functools
import math
from typing import Callable, NamedTuple

import jax
import jax.numpy as jnp
import numpy as np
from jax import lax
from jax.experimental import pallas as pl
from jax.experimental.pallas import tpu as pltpu

F32 = jnp.float32
BF16 = jnp.bfloat16

D_MODEL = 1024
N_META = 16
LN_EPS = 1e-5
RMS_EPS = 1e-6
NEG_BIG = -1e30
LOG2E = 1.4426950408889634

LRU_WIDTH = 256
LRU_BLOCKS = 4
LRU_CONV = 4
LRU_C = 8.0
HEADS = 4
HEAD_DIM = 64
WIDTH = HEADS * HEAD_DIM
MLA_Q_RANK = 256
MLA_KV_RANK = 128
MLA_NOPE = 64
MLA_ROPE = 32
ROPE_BASE = 10000.0
RWKV_LORA = 128
RWKV_IN = 3 * WIDTH + RWKV_LORA
RWKV_GN_EPS = 64e-5
RWKV_DECAY_SCALE = math.exp(-0.5)
N_BRANCH = 4
D_FF = 2816
FFN_CONV = 3

SUBLANES = 8
LANES = 128
MXU_DIM = 256
ROW_TILE = 256
RWKV_CHUNK = 64
RWKV_SUB = 16
FFN_CHUNK = 256
ATTN_KEYS = 1024
ATTN_UNROLL = 2
VMEM_LIMIT = 56 * 1024 * 1024


def _cparams(*sem):
    return pltpu.CompilerParams(dimension_semantics=sem, vmem_limit_bytes=VMEM_LIMIT)


def _whole(shape):
    nd = len(shape)
    return pl.BlockSpec(shape, lambda *_: (0,) * nd)


def _rows(tile, width):
    return pl.BlockSpec((None, tile, width), lambda b, t: (b, t, 0))


def _iota(shape, dim):
    return lax.broadcasted_iota(jnp.int32, shape, dim)


def _dot(a, b):
    return jnp.dot(a, b, preferred_element_type=F32)


def _dot_nt(a, b):
    return lax.dot_general(a, b, (((1,), (1,)), ((), ())), preferred_element_type=F32)


def _dot_tn(a, b):
    return lax.dot_general(a, b, (((0,), (0,)), ((), ())), preferred_element_type=F32)


def _split(x, terms):
    out = []
    for _ in range(terms - 1):
        hi = x.astype(BF16)
        out.append(hi)
        x = x - hi.astype(F32)
    out.append(x.astype(BF16))
    return out


def _mdot(dot, a_terms, b_terms):
    order = max(len(a_terms), len(b_terms))
    acc = None
    for i, a in enumerate(a_terms):
        for j, b in enumerate(b_terms):
            if i + j < order:
                p = dot(a, b)
                acc = p if acc is None else acc + p
    return acc


def _sigmoid(x):
    return 1.0 / (1.0 + jnp.exp(-x))


def _softplus(x):
    return jnp.maximum(x, 0.0) + jnp.log(1.0 + jnp.exp(-jnp.abs(x)))


def _gelu_tanh(x):
    return 0.5 * x * (1.0 + jnp.tanh(math.sqrt(2.0 / math.pi) * (x + 0.044715 * (x * x * x))))


def _layer_norm(x, g, b):
    mu = jnp.mean(x, -1, keepdims=True)
    xc = x - mu
    var = jnp.mean(xc * xc, -1, keepdims=True)
    return xc * lax.rsqrt(var + LN_EPS) * g + b


def _shift_rows(x, d, carry):
    rolled = pltpu.roll(x, d, 0)
    head = rolled[:SUBLANES]
    rows = _iota(head.shape, 0)
    for r in range(d):
        src = SUBLANES - d + r
        head = jnp.where(rows == r, carry[src:src + 1, :], head)
    return jnp.concatenate([head, rolled[SUBLANES:]], axis=0)


def _cumsum_rows(x):
    rows = _iota(x.shape, 0)
    d = 1
    while d < x.shape[0]:
        x = x + jnp.where(rows >= d, pltpu.roll(x, d, 0), 0.0)
        d *= 2
    return x


def _linear_scan_rows(a, u, h_in):
    in_group = _iota(a.shape, 0) % SUBLANES
    d = 1
    while d < SUBLANES:
        keep = in_group >= d
        u = u + a * jnp.where(keep, pltpu.roll(u, d, 0), 0.0)
        a = a * jnp.where(keep, pltpu.roll(a, d, 0), 1.0)
        d *= 2
    groups = []
    for g in range(a.shape[0] // SUBLANES):
        rows = slice(g * SUBLANES, (g + 1) * SUBLANES)
        h = u[rows] + a[rows] * h_in
        h_in = h[SUBLANES - 1:SUBLANES, :]
        groups.append(h)
    return jnp.concatenate(groups, axis=0)


def _tile_lanes(x, n):
    return jnp.concatenate([x] * n, axis=1)


def _head_mask(shape, lane_dim, head):
    lane = _iota(shape, lane_dim)
    return (lane >= head * HEAD_DIM) & (lane < (head + 1) * HEAD_DIM)


def _stack_heads(x):
    return jnp.concatenate(
        [jnp.where(_head_mask(x.shape, 1, h), x, jnp.zeros_like(x)) for h in range(HEADS)], axis=0)


def _head_sum(x, ones_bd):
    return _mdot(_dot, _split(x, 2), [ones_bd])


class RowKernelPart(NamedTuple):
    body: Callable
    operands: list
    operand_specs: list
    out_specs: list
    out_shapes: list
    scratch_shapes: list


def _run_row_kernels(h16, parts, name):
    bsz, lp, d = h16.shape
    n_in = sum(len(p.operands) for p in parts)
    n_out = sum(len(p.out_specs) for p in parts)

    def body(h_ref, *refs):
        ins, outs, scratch = list(refs[:n_in]), list(refs[n_in:n_in + n_out]), list(refs[n_in + n_out:])
        for p in parts:
            take = lambda pool, n: [pool.pop(0) for _ in range(n)]
            p.body(h_ref, *take(ins, len(p.operands)), *take(outs, len(p.out_specs)),
                   *take(scratch, len(p.scratch_shapes)))

    return pl.pallas_call(
        body,
        grid=(bsz, lp // ROW_TILE),
        in_specs=[_rows(ROW_TILE, d)] + [s for p in parts for s in p.operand_specs],
        out_specs=[s for p in parts for s in p.out_specs],
        out_shape=[s for p in parts for s in p.out_shapes],
        scratch_shapes=[s for p in parts for s in p.scratch_shapes],
        compiler_params=_cparams("parallel", "arbitrary"),
        name=name,
    )(h16, *[t for p in parts for t in p.operands])


def _ln_in_kernel(x_ref, meta_ref, g_ref, b_ref, o32_ref, o16_ref, carry_ref, *, t_real):
    t = pl.program_id(1)

    @pl.when(t == 0)
    def _():
        carry_ref[...] = meta_ref[...]

    x = x_ref[...]
    tile = x.shape[0]
    tokens = jnp.concatenate([carry_ref[...], x[:tile - N_META]], axis=0)
    carry_ref[...] = x[tile - N_META:]
    tokens = jnp.where(_iota(tokens.shape, 0) + t * tile < t_real, tokens, 0.0)
    y = _layer_norm(tokens, g_ref[...], b_ref[...])
    o32_ref[...] = y
    o16_ref[...] = y.astype(BF16)


def _ln_in(x, meta, g, b, lp):
    bsz, seq, d = x.shape
    last = seq // ROW_TILE - 1
    return pl.pallas_call(
        functools.partial(_ln_in_kernel, t_real=N_META + seq),
        grid=(bsz, lp // ROW_TILE),
        in_specs=[pl.BlockSpec((None, ROW_TILE, d), lambda b, t: (b, jnp.minimum(t, last), 0)),
                  _whole((N_META, d)), _whole((1, d)), _whole((1, d))],
        out_specs=[_rows(ROW_TILE, d), _rows(ROW_TILE, d)],
        out_shape=[jax.ShapeDtypeStruct((bsz, lp, d), F32), jax.ShapeDtypeStruct((bsz, lp, d), BF16)],
        scratch_shapes=[pltpu.VMEM((N_META, d), F32)],
        compiler_params=_cparams("parallel", "arbitrary"),
        name="ln_in",
    )(x, meta.astype(x.dtype), g.reshape(1, d), b.reshape(1, d))


def _lru_kernel(h_ref, w_ref, cw_ref, cb_ref, wrg_ref, brg_ref, wig_ref, big_ref, lam_ref,
                o_ref, xcarry_ref, hcarry_ref):
    @pl.when(pl.program_id(1) == 0)
    def _():
        xcarry_ref[...] = jnp.zeros_like(xcarry_ref)
        hcarry_ref[...] = jnp.zeros_like(hcarry_ref)

    yx = _dot(h_ref[...], w_ref[...])
    y = yx[:, :LRU_WIDTH]
    x = yx[:, LRU_WIDTH:]
    xcarry = xcarry_ref[...]
    xc = cw_ref[LRU_CONV - 1:LRU_CONV, :] * x + cb_ref[...]
    for d in range(1, LRU_CONV):
        k = LRU_CONV - 1 - d
        xc = xc + cw_ref[k:k + 1, :] * _shift_rows(x, d, xcarry)
    xcarry_ref[...] = x[ROW_TILE - SUBLANES:, :]

    xc16 = xc.astype(BF16)
    r = _sigmoid(_dot(xc16, wrg_ref[...]) + brg_ref[...])
    i = _sigmoid(_dot(xc16, wig_ref[...]) + big_ref[...])
    log_a = (-LRU_C) * r * _softplus(-lam_ref[...])
    a = jnp.exp(log_a)
    th = jnp.tanh(log_a)
    u = jnp.sqrt(-2.0 * th / (1.0 - th)) * (i * xc)
    h = _linear_scan_rows(a, u, hcarry_ref[SUBLANES - 1:SUBLANES, :])
    hcarry_ref[...] = h[ROW_TILE - SUBLANES:, :]
    o_ref[...] = (_gelu_tanh(y) * h).astype(BF16)


def _block_diag(w):
    n, d, e = w.shape
    eye = jnp.eye(n, dtype=w.dtype)
    return (eye[:, None, :, None] * w[:, :, None, :]).reshape(n * d, n * e)


def _lru_branch(h16, w_yx, conv_w, conv_b, w_rg, b_rg, w_ig, b_ig, lam):
    bsz, lp, d = h16.shape
    row = lambda v: v.reshape(1, LRU_WIDTH)
    operands = [w_yx.astype(BF16), conv_w, row(conv_b), _block_diag(w_rg).astype(BF16), row(b_rg),
                _block_diag(w_ig).astype(BF16), row(b_ig), row(lam)]
    return RowKernelPart(
        _lru_kernel, operands, [_whole(t.shape) for t in operands],
        [_rows(ROW_TILE, LRU_WIDTH)], [jax.ShapeDtypeStruct((bsz, lp, LRU_WIDTH), BF16)],
        [pltpu.VMEM((SUBLANES, LRU_WIDTH), F32), pltpu.VMEM((SUBLANES, LRU_WIDTH), F32)])


def _values_with_ones(v_spread):
    lane = _iota(v_spread.shape, 1)
    return jnp.where(lane % LANES >= HEAD_DIM, 1.0, v_spread).astype(BF16)


def _spread_heads(w):
    k = w.shape[0]
    return jnp.pad(w.reshape(k, HEADS, HEAD_DIM), ((0, 0), (0, 0), (0, LANES - HEAD_DIM))).reshape(k, HEADS * LANES)


def _fox_proj_kernel(h_ref, w_ref, bf_ref, q_ref, k_ref, v_ref, f_ref, fcarry_ref, *, q_scale):
    @pl.when(pl.program_id(1) == 0)
    def _():
        fcarry_ref[...] = jnp.zeros_like(fcarry_ref)

    z = _dot(h_ref[...], w_ref[...])
    q_ref[...] = (z[:, 0:WIDTH] * q_scale).astype(BF16)
    k_ref[...] = z[:, WIDTH:2 * WIDTH].astype(BF16)
    v_ref[...] = _values_with_ones(z[:, 2 * WIDTH:2 * WIDTH + HEADS * LANES])
    log_f = -_softplus(-(z[:, 2 * WIDTH + HEADS * LANES:] + bf_ref[...]))
    f = _cumsum_rows(log_f) + fcarry_ref[SUBLANES - 1:SUBLANES, :]
    fcarry_ref[...] = f[ROW_TILE - SUBLANES:, :]
    f_ref[...] = f


def _fox_proj(h16, w_qkvf, b_f, q_scale):
    bsz, lp, d = h16.shape
    bf = jnp.zeros((1, LANES), F32).at[0, :HEADS].set(b_f)
    act = lambda w: jax.ShapeDtypeStruct((bsz, lp, w), BF16)
    return RowKernelPart(
        functools.partial(_fox_proj_kernel, q_scale=q_scale),
        [w_qkvf.astype(BF16), bf], [_whole(w_qkvf.shape), _whole((1, LANES))],
        [_rows(ROW_TILE, WIDTH), _rows(ROW_TILE, WIDTH), _rows(ROW_TILE, HEADS * LANES), _rows(ROW_TILE, LANES)],
        [act(WIDTH), act(WIDTH), act(HEADS * LANES), jax.ShapeDtypeStruct((bsz, lp, LANES), F32)],
        [pltpu.VMEM((SUBLANES, LANES), F32)])


def _rms_norm(x, g):
    return x * lax.rsqrt(jnp.mean(x * x, -1, keepdims=True) + RMS_EPS) * g


def _mla_proj_kernel(h_ref, win_ref, qg_ref, kvg_ref, wq_ref, wkv_ref, cos_ref, sin_ref,
                     qn_ref, qr_ref, k01_ref, k23_ref, v_ref, *, q_scale):
    z = _dot(h_ref[...], win_ref[...])
    c_q = z[:, 0:MLA_Q_RANK]
    c_kv = z[:, MLA_Q_RANK:MLA_Q_RANK + MLA_KV_RANK]
    k_r = z[:, MLA_Q_RANK + MLA_KV_RANK:MLA_Q_RANK + MLA_KV_RANK + LANES]
    k_r_swapped = z[:, MLA_Q_RANK + MLA_KV_RANK + LANES:]
    cos = cos_ref[...]
    sin = sin_ref[...]

    q = _dot(_rms_norm(c_q, qg_ref[...]).astype(BF16), wq_ref[...])
    qn_ref[...] = (q[:, 0:WIDTH] * q_scale).astype(BF16)
    rope_w = HEADS * LANES
    for h in range(HEADS):
        plain = q[:, WIDTH + h * LANES:WIDTH + (h + 1) * LANES]
        swapped = q[:, WIDTH + rope_w + h * LANES:WIDTH + rope_w + (h + 1) * LANES]
        qr_ref[:, h * LANES:(h + 1) * LANES] = ((plain * cos + swapped * sin) * q_scale).astype(BF16)

    kv = _dot(_rms_norm(c_kv, kvg_ref[...]).astype(BF16), wkv_ref[...])
    k_rope = (k_r * cos + k_r_swapped * sin).astype(BF16)
    for g, k_ref in enumerate((k01_ref, k23_ref)):
        k_ref[:, 0:LANES] = kv[:, g * LANES:(g + 1) * LANES].astype(BF16)
        k_ref[:, LANES:] = k_rope
    v_ref[...] = _values_with_ones(kv[:, WIDTH:])


def _swap_halves(w):
    half = w.shape[-1] // 2
    return jnp.concatenate([w[..., half:], w[..., :half]], axis=-1)


def _pad_cols(w, width):
    return jnp.pad(w, ((0, 0), (0, width - w.shape[1])))


def _take_cols_kernel(a_ref, b_ref, o_ref, *, shift):
    both = jnp.concatenate([a_ref[...], b_ref[...]], axis=1)
    o_ref[...] = both[:, shift:shift + o_ref.shape[1]].astype(BF16)


def _take_cols(w, layer, start, width, block):
    rows = w.shape[1]
    first, shift = divmod(start, block)
    assert (first + width // block) * block < w.shape[2]
    return pl.pallas_call(
        functools.partial(_take_cols_kernel, shift=shift),
        grid=(width // block,),
        in_specs=[pl.BlockSpec((None, rows, block), lambda j: (layer, 0, j + first)),
                  pl.BlockSpec((None, rows, block), lambda j: (layer, 0, j + first + 1))],
        out_specs=pl.BlockSpec((rows, block), lambda j: (0, j)),
        out_shape=jax.ShapeDtypeStruct((rows, width), BF16),
        compiler_params=_cparams("parallel"),
        name="take_cols",
    )(w, w)


def _mla_proj(h16, w_cq, w_ckv, w_kr, q_norm_g, kv_norm_g, w_uq, w_ukv, cos_t, sin_t, q_scale):
    bsz, lp, d = h16.shape
    w_in = jnp.concatenate([w_cq, w_ckv, _pad_cols(w_kr, LANES), _pad_cols(_swap_halves(w_kr), LANES)], axis=1)
    uq = w_uq.reshape(MLA_Q_RANK, HEADS, MLA_NOPE + MLA_ROPE)
    q_nope = uq[:, :, :MLA_NOPE].reshape(MLA_Q_RANK, WIDTH)
    q_rope = uq[:, :, MLA_NOPE:]
    lane_pad = lambda r: jnp.pad(r, ((0, 0), (0, 0), (0, LANES - MLA_ROPE))).reshape(MLA_Q_RANK, HEADS * LANES)
    wq = jnp.concatenate([q_nope, lane_pad(q_rope), lane_pad(_swap_halves(q_rope))], axis=1)
    ukv = w_ukv.reshape(MLA_KV_RANK, HEADS, MLA_NOPE + HEAD_DIM)
    wkv = jnp.concatenate([ukv[:, :, :MLA_NOPE].reshape(MLA_KV_RANK, WIDTH),
                           _spread_heads(ukv[:, :, MLA_NOPE:].reshape(MLA_KV_RANK, WIDTH))], axis=1)
    table = pl.BlockSpec((ROW_TILE, LANES), lambda b, t: (t, 0))
    act = lambda w: jax.ShapeDtypeStruct((bsz, lp, w), BF16)
    return RowKernelPart(
        functools.partial(_mla_proj_kernel, q_scale=q_scale),
        [w_in.astype(BF16), q_norm_g.reshape(1, -1), kv_norm_g.reshape(1, -1), wq.astype(BF16), wkv.astype(BF16),
         cos_t, sin_t],
        [_whole(w_in.shape), _whole((1, MLA_Q_RANK)), _whole((1, MLA_KV_RANK)), _whole(wq.shape), _whole(wkv.shape),
         table, table],
        [_rows(ROW_TILE, WIDTH), _rows(ROW_TILE, HEADS * LANES), _rows(ROW_TILE, WIDTH), _rows(ROW_TILE, WIDTH),
         _rows(ROW_TILE, HEADS * LANES)],
        [act(WIDTH), act(HEADS * LANES), act(WIDTH), act(WIDTH), act(HEADS * LANES)],
        [])


def _rotary_tables(lp):
    inv = 1.0 / (ROPE_BASE ** (jnp.arange(0, MLA_ROPE, 2, dtype=F32) / MLA_ROPE))
    ang = jnp.arange(lp, dtype=F32)[:, None] * inv[None, :]
    cos, sin = jnp.cos(ang), jnp.sin(ang)
    pad = jnp.zeros((lp, LANES - MLA_ROPE), F32)
    return (jnp.concatenate([cos, cos, pad], axis=1), jnp.concatenate([-sin, sin, pad], axis=1))


def _attn_kernel(*refs, has_rope, has_bias):
    it = iter(refs)
    q_ref = next(it)
    qr_ref = next(it) if has_rope else None
    k_refs = [next(it) for _ in range(2 if has_rope else 1)]
    v_ref = next(it)
    f_ref = next(it) if has_bias else None
    o_ref = next(it)
    m_ref, acc_ref = next(it), next(it)

    tq = q_ref.shape[0]
    qi = pl.program_id(1)
    heads_per_group = HEADS // len(k_refs)
    q_stacks = []
    for g in range(len(k_refs)):
        parts = []
        for h in range(g * heads_per_group, (g + 1) * heads_per_group):
            if has_rope:
                qg = q_ref[:, g * LANES:(g + 1) * LANES]
                qh = jnp.where(_head_mask(qg.shape, 1, h % heads_per_group), qg, jnp.zeros_like(qg))
                qh = jnp.concatenate([qh, qr_ref[:, h * LANES:(h + 1) * LANES]], axis=1)
            else:
                q = q_ref[...]
                qh = jnp.where(_head_mask(q.shape, 1, h), q, jnp.zeros_like(q))
            parts.append(qh)
        q_stacks.append(jnp.concatenate(parts, axis=0))

    m_ref[...] = jnp.full(m_ref.shape, NEG_BIG, F32)
    acc_ref[...] = jnp.zeros_like(acc_ref)
    if has_bias:
        f_q0 = f_ref[:, pl.ds(pl.multiple_of(qi * tq, tq), LANES)][:, 0:1]

    def block(start, tk, masked):
        scores = [_dot_nt(qs, k_ref[pl.ds(start, tk), :]) for qs, k_ref in zip(q_stacks, k_refs)]
        if has_bias:
            bias = (f_q0 - f_ref[:, pl.ds(start, tk)]) * LOG2E
        if masked:
            keep = _iota((tq, tk), 0) >= _iota((tq, tk), 1)
        for h in range(HEADS):
            rows = slice(h * tq, (h + 1) * tq)
            in_group = h % heads_per_group
            sh = scores[h // heads_per_group][in_group * tq:(in_group + 1) * tq]
            if has_bias:
                sh = sh + bias[h:h + 1, :]
            if masked:
                sh = jnp.where(keep, sh, NEG_BIG)
            m_old = m_ref[rows]
            m_new = jnp.maximum(m_old, jnp.max(sh, axis=1, keepdims=True))
            p = jnp.exp2(sh - _tile_lanes(m_new, tk // LANES))
            alpha = jnp.exp2(m_old - m_new)
            m_ref[rows] = m_new
            v_ones = v_ref[pl.ds(start, tk), h * LANES:(h + 1) * LANES]
            acc_ref[rows] = acc_ref[rows] * alpha + _dot(p.astype(BF16), v_ones)

    per = ATTN_KEYS // tq
    step_keys = ATTN_UNROLL * ATTN_KEYS
    n_steps = (qi * tq) // step_keys

    def body(j, carry):
        base = pl.multiple_of(j * step_keys, step_keys)
        for u in range(ATTN_UNROLL):
            block(base + u * ATTN_KEYS, ATTN_KEYS, False)
        return carry

    lax.fori_loop(0, n_steps, body, 0)
    left = qi - n_steps * (step_keys // tq)
    rest = pl.multiple_of(n_steps * step_keys, step_keys)
    for u in range(ATTN_UNROLL - 1):
        @pl.when(left >= (u + 1) * per)
        def _():
            block(rest + u * ATTN_KEYS, ATTN_KEYS, False)
    tail = left % per
    width, pos = per // 2, qi - tail
    while width >= 1:
        @pl.when(tail & width != 0)
        def _(width=width, pos=pos):
            block(pl.multiple_of(pos * tq, tq), width * tq, False)
        pos = pos + (tail & width)
        width //= 2

    block(pl.multiple_of(qi * tq, tq), tq, True)

    first_half = _iota((tq, LANES), 1) < HEAD_DIM
    heads_out = []
    for h in range(HEADS):
        acc = acc_ref[h * tq:(h + 1) * tq]
        heads_out.append(acc / pltpu.roll(acc, HEAD_DIM, 1))
    for g in range(HEADS // 2):
        pair = jnp.where(first_half, heads_out[2 * g], pltpu.roll(heads_out[2 * g + 1], HEAD_DIM, 1))
        o_ref[:, g * LANES:(g + 1) * LANES] = pair.astype(BF16)


def _attention(q, keys, v_ones, q_rope=None, f_rows=None):
    bsz, lp, _ = q.shape
    tq = ROW_TILE
    resident = lambda w: pl.BlockSpec((None, lp, w), lambda b, t: (b, 0, 0))
    in_specs = [_rows(tq, WIDTH)]
    args = [q]
    if q_rope is not None:
        in_specs.append(_rows(tq, HEADS * LANES))
        args.append(q_rope)
    in_specs += [resident(WIDTH)] * len(keys) + [resident(HEADS * LANES)]
    args += [*keys, v_ones]
    if f_rows is not None:
        in_specs.append(pl.BlockSpec((None, SUBLANES, lp), lambda b, t: (b, 0, 0)))
        args.append(f_rows)
    kern = functools.partial(_attn_kernel, has_rope=q_rope is not None,
                             has_bias=f_rows is not None)
    return pl.pallas_call(
        kern,
        grid=(bsz, lp // tq),
        in_specs=in_specs,
        out_specs=_rows(tq, WIDTH),
        out_shape=jax.ShapeDtypeStruct((bsz, lp, WIDTH), BF16),
        scratch_shapes=[pltpu.VMEM((HEADS * tq, LANES), F32), pltpu.VMEM((HEADS * tq, LANES), F32)],
        compiler_params=_cparams("parallel", "arbitrary"),
        name="attn_bias" if f_rows is not None else "attn_rope",
    )(*args)


def _bf(x):
    return x.astype(BF16)


def _stack_bf(x):
    return _bf(_stack_heads(x))


def _unit_lower_inverse(ms, sub_blk, eye):
    n = HEADS * RWKV_CHUNK
    eye_bd = (_iota((n, n), 0) == _iota((n, n), 1)).astype(F32)
    md = [jnp.where(sub_blk, m, 0.0) for m in ms]
    mo = [_stack_bf(m - d) for m, d in zip(ms, md)]
    x = [eye + d for d in md]
    p = md
    p_bd = [_stack_heads(t) for t in p]
    for _ in range(int(math.log2(RWKV_SUB)) - 1):
        p = [_dot(_bf(t), _bf(s)) for t, s in zip(p, p_bd)]
        p_bd = [_stack_heads(t) for t in p]
        x = [_dot(_bf(a), _bf(eye_bd + s)) for a, s in zip(x, p_bd)]
    x_bd = [_stack_bf(a) for a in x]
    pm = [_dot(_bf(a), o) for a, o in zip(x, mo)]
    pm2 = [_dot(_bf(t), _stack_bf(t)) for t in pm]
    pm3 = [_dot(_bf(a), _stack_bf(b)) for a, b in zip(pm, pm2)]
    return [_dot(_bf(eye + a + b + c), s) for a, b, c, s in zip(pm, pm2, pm3, x_bd)]


def _rwkv_kernel(h_ref, w_ref, mu_ref, w0_ref, w2_ref, a0_ref, a2_ref, g2_ref, kk_ref, ka_ref,
                 rk_ref, gng_ref, gnb_ref, ones_ref, o_ref, pcarry_ref, state_ref, y_ref):
    @pl.when(pl.program_id(0) == 0)
    def _():
        pcarry_ref[...] = jnp.zeros_like(pcarry_ref)
        state_ref[...] = jnp.zeros_like(state_ref)

    bsz, tile, _ = h_ref.shape
    ones_bd = ones_ref[...]
    wide = (RWKV_CHUNK, HEADS * RWKV_CHUNK)
    t_idx, s_idx = _iota(wide, 0), _iota(wide, 1) % RWKV_CHUNK
    strict = t_idx > s_idx
    incl = t_idx >= s_idx
    sub_blk = (t_idx // RWKV_SUB) == (s_idx // RWKV_SUB)
    eye = (t_idx == s_idx).astype(F32)
    diag = _iota((WIDTH, WIDTH), 0) == _iota((WIDTH, WIDTH), 1)
    mid = RWKV_CHUNK // 2 - 1
    chunk_rows = [slice(c * RWKV_CHUNK, (c + 1) * RWKV_CHUNK) for c in range(tile // RWKV_CHUNK)]

    seqs = []
    ar_t, b_t, k_t, a_0, r_0, b_e, k_e, v_s, w_end = ([] for _ in range(9))
    for b in range(bsz):
        p_in = _dot(h_ref[b], w_ref[...])
        prev = _shift_rows(p_in, 1, pcarry_ref[b])
        pcarry_ref[b] = p_in[tile - SUBLANES:, :]
        p = p_in + (prev - p_in) * mu_ref[...]
        r = p[:, 0:WIDTH]
        k = p[:, WIDTH:2 * WIDTH]
        v = p[:, 2 * WIDTH:3 * WIDTH]
        lora = p[:, 3 * WIDTH:]
        z = w0_ref[...] + _dot(jnp.tanh(lora).astype(BF16), w2_ref[...])
        log_w = (-RWKV_DECAY_SCALE) * _sigmoid(z)
        a_gate = _sigmoid(a0_ref[...] + _dot(lora.astype(BF16), a2_ref[...]))
        gate = _dot(_sigmoid(lora).astype(BF16), g2_ref[...])
        kk = k * kk_ref[...]
        kk = kk * lax.rsqrt(jnp.maximum(_head_sum(kk * kk, ones_bd), 1e-24))
        kf = k * (1.0 + (a_gate - 1.0) * ka_ref[...])
        a_vec = -kk
        b_vec = kk * a_gate
        seqs.append((r, kf, v, gate))
        for rows in chunk_rows:
            lw = log_w[rows]
            g_in = _cumsum_rows(lw)
            g_ex = g_in - lw
            g_mid = g_in[mid:mid + 1, :]
            g_end = g_in[RWKV_CHUNK - 1:RWKV_CHUNK, :]
            decay_out = jnp.exp(g_mid - g_in)
            to_end = jnp.exp(g_end - g_in)
            a_c, b_c, k_c, r_c = a_vec[rows], b_vec[rows], kf[rows], r[rows]
            ar_t.append(_bf(jnp.concatenate([a_c * jnp.exp(g_ex - g_mid), r_c * jnp.exp(g_in - g_mid)], axis=0)))
            b_t.append(_stack_bf(b_c * decay_out))
            k_t.append(_stack_bf(k_c * decay_out))
            a_0.append(_stack_bf(a_c * jnp.exp(g_ex)))
            r_0.append(r_c * jnp.exp(g_in))
            b_e.append(_stack_bf(b_c * to_end))
            k_e.append(_stack_bf(k_c * to_end))
            v_s.append(_stack_bf(v[rows]))
            w_end.append(jnp.exp(g_end))

    each = lambda f, *lists: [f(*args) for args in zip(*lists)]
    top, bottom = slice(0, RWKV_CHUNK), slice(RWKV_CHUNK, 2 * RWKV_CHUNK)
    gram_b = each(_dot_nt, ar_t, b_t)
    gram_k = each(_dot_nt, ar_t, k_t)
    m_ab = each(lambda g: jnp.where(strict, g[top], 0.0), gram_b)
    m_ak = each(lambda g: _bf(jnp.where(strict, g[top], 0.0)), gram_k)
    m_rb = each(lambda g: _bf(jnp.where(incl, g[bottom], 0.0)), gram_b)
    m_rk = each(lambda g: _bf(jnp.where(incl, g[bottom], 0.0)), gram_k)
    t_inv = each(_bf, _unit_lower_inverse(m_ab, sub_blk, eye))
    w_a = each(lambda t, a: _stack_bf(_dot(t, a)), t_inv, a_0)
    m_v = each(lambda m, x: _stack_bf(_dot(m, x)), m_ak, v_s)
    u_v = each(lambda t, x: _stack_bf(_dot(t, x)), t_inv, m_v)
    r_eff = each(lambda r0, m, x: r0 + _dot(m, x), r_0, m_rb, w_a)
    y_v = each(lambda mb, u, mk, x: _dot(mb, u) + _dot(mk, x), m_rb, u_v, m_rk, v_s)
    g_mat = each(lambda w, b, x: jnp.where(diag, w, 0.0) + _dot_tn(b, x), w_end, b_e, w_a)
    d_mat = each(lambda b, u, kx, x: _dot_tn(b, u) + _dot_tn(kx, x), b_e, u_v, k_e, v_s)

    for c, rows in enumerate(chunk_rows):
        for b in range(bsz):
            i = b * len(chunk_rows) + c
            state = _split(state_ref[b], 2)
            y_ref[b, rows, :] = _mdot(_dot, _split(r_eff[i], 2), state) + y_v[i]
            state_ref[b] = _mdot(_dot, _split(g_mat[i], 2), state) + d_mat[i]

    inv_n = 1.0 / HEAD_DIM
    for b, (r, kf, v, gate) in enumerate(seqs):
        y = y_ref[b]
        y_mu = _head_sum(y, ones_bd) * inv_n
        yc = y - y_mu
        y_var = _head_sum(yc * yc, ones_bd) * inv_n
        y = yc * lax.rsqrt(y_var + RWKV_GN_EPS) * gng_ref[...] + gnb_ref[...]
        y = y + _head_sum(r * kf * rk_ref[...], ones_bd) * v
        o_ref[b] = (y * gate).astype(BF16)


def _rwkv_branch(h16, w_p, mu, w0, w2, a0, a2, g2, k_k, k_a, r_k, gn_g, gn_b):
    bsz, lp, d = h16.shape
    row = lambda t: t.reshape(1, -1)
    lora_rows = lambda w, lo: jnp.zeros((RWKV_LORA, WIDTH), F32).at[lo:lo + w.shape[0]].set(w).astype(BF16)
    ones_bd = _block_diag(jnp.ones((HEADS, HEAD_DIM, HEAD_DIM), F32)).astype(BF16)
    params = [w_p.astype(BF16), row(mu), row(w0), lora_rows(w2, 0), row(a0), lora_rows(a2, 32),
              lora_rows(g2, 64), row(k_k), row(k_a), row(r_k), row(gn_g), row(gn_b), ones_bd]
    all_seqs = lambda w: pl.BlockSpec((bsz, ROW_TILE, w), lambda t: (0, t, 0))
    return pl.pallas_call(
        _rwkv_kernel,
        grid=(lp // ROW_TILE,),
        in_specs=[all_seqs(d)] + [_whole(t.shape) for t in params],
        out_specs=all_seqs(WIDTH),
        out_shape=jax.ShapeDtypeStruct((bsz, lp, WIDTH), BF16),
        scratch_shapes=[pltpu.VMEM((bsz, SUBLANES, RWKV_IN), F32), pltpu.VMEM((bsz, WIDTH, WIDTH), F32),
                        pltpu.VMEM((bsz, ROW_TILE, WIDTH), F32)],
        compiler_params=_cparams("arbitrary"),
        name="rwkv",
    )(h16, *params)


def _merge_kernel(h32_ref, h16_ref, oa_ref, ob_ref, oc_ref, od_ref, wg_ref, wb_ref, wo_ref, g_ref, b_ref,
                  o32_ref, o16_ref, *, alpha):
    h16 = h16_ref[...]
    acc = None
    for n, o_ref in enumerate((oa_ref, ob_ref, oc_ref, od_ref)):
        gate = _sigmoid(_dot(h16, wg_ref[:, n * D_MODEL:(n + 1) * D_MODEL]))
        term = gate * _dot(o_ref[...], wb_ref[n])
        acc = term if acc is None else acc + term
    mixed = _dot(acc.astype(BF16), wo_ref[...])
    y = _layer_norm(alpha * h32_ref[...] + mixed, g_ref[...], b_ref[...])
    o32_ref[...] = y
    o16_ref[...] = y.astype(BF16)


def _merge(h32, h16, branches, w_gate, w_branch, w_out, ln_g, ln_b, alpha):
    bsz, lp, d = h32.shape
    return pl.pallas_call(
        functools.partial(_merge_kernel, alpha=alpha),
        grid=(bsz, lp // ROW_TILE),
        in_specs=[_rows(ROW_TILE, d), _rows(ROW_TILE, d)] + [_rows(ROW_TILE, WIDTH)] * N_BRANCH
        + [_whole((d, N_BRANCH * d)), _whole((N_BRANCH, WIDTH, d)), _whole((d, d)), _whole((1, d)), _whole((1, d))],
        out_specs=[_rows(ROW_TILE, d), _rows(ROW_TILE, d)],
        out_shape=[jax.ShapeDtypeStruct(h32.shape, F32), jax.ShapeDtypeStruct(h32.shape, BF16)],
        compiler_params=_cparams("parallel", "parallel"),
        name="merge",
    )(h32, h16, *branches, w_gate.astype(BF16), w_branch.astype(BF16), w_out.astype(BF16),
      ln_g.reshape(1, d), ln_b.reshape(1, d))


def _ffn_kernel(h32_ref, h16_ref, wu_ref, cw_ref, cb_ref, wd_ref, g_ref, b_ref, *rest, alpha, last_layer):
    if last_layer:
        out_ref, carry_ref, act_ref, prev_ref = rest
    else:
        o32_ref, o16_ref, carry_ref, act_ref = rest

    @pl.when(pl.program_id(1) == 0)
    def _():
        carry_ref[...] = jnp.zeros_like(carry_ref)
        if last_layer:
            prev_ref[...] = jnp.zeros_like(prev_ref)

    h16 = h16_ref[...]

    def conv_cols(lo):
        cols = slice(lo, lo + FFN_CHUNK)
        x = _dot(h16, wu_ref[:, cols])
        carry = carry_ref[:, cols]
        out = cw_ref[FFN_CONV - 1:FFN_CONV, cols] * x + cb_ref[:, cols]
        for d in range(1, FFN_CONV):
            k = FFN_CONV - 1 - d
            out = out + cw_ref[k:k + 1, cols] * _shift_rows(x, d, carry)
        carry_ref[:, cols] = x[ROW_TILE - SUBLANES:, :]
        return out

    for j in range(D_FF // FFN_CHUNK):
        gate = conv_cols(j * FFN_CHUNK)
        val = conv_cols(D_FF + j * FFN_CHUNK)
        act_ref[:, j * FFN_CHUNK:(j + 1) * FFN_CHUNK] = (gate * _sigmoid(gate) * val).astype(BF16)
    y = _layer_norm(alpha * h32_ref[...] + _dot(act_ref[...], wd_ref[...]), g_ref[...], b_ref[...])
    if last_layer:
        out_ref[...] = jnp.concatenate([prev_ref[N_META:, :], y[:N_META]], axis=0)
        prev_ref[...] = y
    else:
        o32_ref[...] = y
        o16_ref[...] = y.astype(BF16)


def _ffn(h32, h16, w_up, conv_w, conv_b, w_down, ln_g, ln_b, alpha, last_layer):
    bsz, lp, d = h32.shape
    params = [w_up.astype(BF16), jnp.pad(conv_w, ((0, SUBLANES - FFN_CONV), (0, 0))), conv_b.reshape(1, -1),
              w_down.astype(BF16), ln_g.reshape(1, d), ln_b.reshape(1, d)]
    scratch = [pltpu.VMEM((SUBLANES, 2 * D_FF), F32), pltpu.VMEM((ROW_TILE, D_FF), BF16)]
    if last_layer:
        out_specs = pl.BlockSpec((None, ROW_TILE, d), lambda b, t: (b, jnp.maximum(t - 1, 0), 0))
        out_shape = jax.ShapeDtypeStruct((bsz, lp - ROW_TILE, d), F32)
        scratch.append(pltpu.VMEM((ROW_TILE, d), F32))
    else:
        out_specs = [_rows(ROW_TILE, d), _rows(ROW_TILE, d)]
        out_shape = [jax.ShapeDtypeStruct(h32.shape, F32), jax.ShapeDtypeStruct(h32.shape, BF16)]
    return pl.pallas_call(
        functools.partial(_ffn_kernel, alpha=alpha, last_layer=last_layer),
        grid=(bsz, lp // ROW_TILE),
        in_specs=[_rows(ROW_TILE, d), _rows(ROW_TILE, d)] + [_whole(t.shape) for t in params],
        out_specs=out_specs,
        out_shape=out_shape,
        scratch_shapes=scratch,
        compiler_params=_cparams("parallel", "arbitrary"),
        name="ffn_last" if last_layer else "ffn",
    )(h32, h16, *params)


def kernel(x, meta_tokens, ln_in_g, ln_in_b, w_in, w_branch, w_out, ln_mix_g, ln_mix_b, lru_conv_w, lru_conv_b, lru_w_rg, lru_b_rg, lru_w_ig, lru_b_ig, lru_lambda, fox_b_f, mla_q_norm_g, mla_kv_norm_g, mla_w_uq, mla_w_ukv, rwkv_mu, rwkv_w0, rwkv_w2, rwkv_a0, rwkv_a2, rwkv_g2, rwkv_k_k, rwkv_k_a, rwkv_r_k, rwkv_gn_g, rwkv_gn_b, ffn_w_up, ffn_conv_w, ffn_conv_b, ffn_w_down, ln_ffn_g, ln_ffn_b):
    bsz, seq, d = x.shape
    depth = w_in.shape[0]
    alpha = (2.0 * depth) ** 0.25
    assert seq % ROW_TILE == 0 and meta_tokens.shape == (N_META, d)
    lp = seq + ROW_TILE
    h32, h16 = _ln_in(x, meta_tokens, ln_in_g, ln_in_b, lp)
    cos_t, sin_t = _rotary_tables(lp)

    widths = (LRU_WIDTH, LRU_WIDTH, WIDTH, WIDTH, WIDTH, HEADS, MLA_Q_RANK, MLA_KV_RANK, MLA_ROPE,
              RWKV_IN, N_BRANCH * D_MODEL)
    offs = np.concatenate([[0], np.cumsum(widths)])
    col = lambda l, i, j=None: w_in[l, :, int(offs[i]):int(offs[(i if j is None else j) + 1])]

    for l in range(depth):
        w_mla = _take_cols(w_in, l,int(offs[6]), 4 * LANES, 4 * LANES)
        w_cq, w_ckv = w_mla[:, :MLA_Q_RANK], w_mla[:, MLA_Q_RANK:MLA_Q_RANK + MLA_KV_RANK]
        w_kr = w_mla[:, MLA_Q_RANK + MLA_KV_RANK:MLA_Q_RANK + MLA_KV_RANK + MLA_ROPE]
        w_tail = _take_cols(w_in, l,int(offs[9]), RWKV_IN + N_BRANCH * D_MODEL, 3 * LANES)
        w_fox = jnp.concatenate([col(l,2, 3), _spread_heads(col(l,4)), _pad_cols(col(l,5), LANES)], axis=1)
        o_a, fq, fk, fv, f_cum, mq, mqr, mk01, mk23, mv = _run_row_kernels(h16, [
            _lru_branch(h16, col(l,0, 1), lru_conv_w[l], lru_conv_b[l], lru_w_rg[l], lru_b_rg[l],
                        lru_w_ig[l], lru_b_ig[l], lru_lambda[l]),
            _fox_proj(h16, w_fox, fox_b_f[l], LOG2E / math.sqrt(HEAD_DIM)),
            _mla_proj(h16, w_cq, w_ckv, w_kr, mla_q_norm_g[l], mla_kv_norm_g[l], mla_w_uq[l], mla_w_ukv[l],
                      cos_t, sin_t, LOG2E / math.sqrt(MLA_NOPE + MLA_ROPE))], "mixer_proj")
        f_rows = jnp.swapaxes(f_cum[:, :, :SUBLANES], 1, 2)
        o_b = _attention(fq, (fk,), fv, f_rows=f_rows)
        o_c = _attention(mq, (mk01, mk23), mv, q_rope=mqr)
        o_d = _rwkv_branch(h16, w_tail[:, :RWKV_IN], rwkv_mu[l], rwkv_w0[l], rwkv_w2[l], rwkv_a0[l], rwkv_a2[l],
                           rwkv_g2[l], rwkv_k_k[l], rwkv_k_a[l], rwkv_r_k[l], rwkv_gn_g[l], rwkv_gn_b[l])
        h32, h16 = _merge(h32, h16, (o_a, o_b, o_c, o_d), w_tail[:, RWKV_IN:], w_branch[l], w_out[l],
                          ln_mix_g[l], ln_mix_b[l], alpha)
        out = _ffn(h32, h16, ffn_w_up[l], ffn_conv_w[l], ffn_conv_b[l], ffn_w_down[l],
                   ln_ffn_g[l], ln_ffn_b[l], alpha, last_layer=l == depth - 1)
        if l < depth - 1:
            h32, h16 = out
    return out
```

```python
import functools
import math
from typing import Callable, NamedTuple

import jax
import jax.numpy as jnp
import numpy as np
from jax import lax
from jax.experimental import pallas as pl
from jax.experimental.pallas import tpu as pltpu

F32 = jnp.float32
BF16 = jnp.bfloat16

D_MODEL = 1024
N_META = 16
LN_EPS = 1e-5
RMS_EPS = 1e-6
NEG_BIG = -1e30
LOG2E = 1.4426950408889634

LRU_WIDTH = 256
LRU_BLOCKS = 4
LRU_CONV = 4
LRU_C = 8.0
HEADS = 4
HEAD_DIM = 64
WIDTH = HEADS * HEAD_DIM
MLA_Q_RANK = 256
MLA_KV_RANK = 128
MLA_NOPE = 64
MLA_ROPE = 32
ROPE_BASE = 10000.0
RWKV_LORA = 128
RWKV_IN = 3 * WIDTH + RWKV_LORA
RWKV_GN_EPS = 64e-5
RWKV_DECAY_SCALE = math.exp(-0.5)
N_BRANCH = 4
D_FF = 2816
FFN_CONV = 3

SUBLANES = 8
LANES = 128
MXU_DIM = 256
ROW_TILE = 256
RWKV_CHUNK = 64
RWKV_SUB = 16
FFN_CHUNK = 256
ATTN_KEYS = 1024
ATTN_UNROLL = 2
VMEM_LIMIT = 56 * 1024 * 1024


def _cparams(*sem):
    return pltpu.CompilerParams(dimension_semantics=sem, vmem_limit_bytes=VMEM_LIMIT)


def _whole(shape):
    nd = len(shape)
    return pl.BlockSpec(shape, lambda *_: (0,) * nd)


def _rows(tile, width):
    return pl.BlockSpec((None, tile, width), lambda b, t: (b, t, 0))


def _iota(shape, dim):
    return lax.broadcasted_iota(jnp.int32, shape, dim)


def _dot(a, b):
    return jnp.dot(a, b, preferred_element_type=F32)


def _dot_nt(a, b):
    return lax.dot_general(a, b, (((1,), (1,)), ((), ())), preferred_element_type=F32)


def _dot_tn(a, b):
    return lax.dot_general(a, b, (((0,), (0,)), ((), ())), preferred_element_type=F32)


def _split(x, terms):
    out = []
    for _ in range(terms - 1):
        hi = x.astype(BF16)
        out.append(hi)
        x = x - hi.astype(F32)
    out.append(x.astype(BF16))
    return out


def _mdot(dot, a_terms, b_terms):
    order = max(len(a_terms), len(b_terms))
    acc = None
    for i, a in enumerate(a_terms):
        for j, b in enumerate(b_terms):
            if i + j < order:
                p = dot(a, b)
                acc = p if acc is None else acc + p
    return acc


def _sigmoid(x):
    return 1.0 / (1.0 + jnp.exp(-x))


def _softplus(x):
    return jnp.maximum(x, 0.0) + jnp.log(1.0 + jnp.exp(-jnp.abs(x)))


def _gelu_tanh(x):
    return 0.5 * x * (1.0 + jnp.tanh(math.sqrt(2.0 / math.pi) * (x + 0.044715 * (x * x * x))))


def _layer_norm(x, g, b):
    mu = jnp.mean(x, -1, keepdims=True)
    xc = x - mu
    var = jnp.mean(xc * xc, -1, keepdims=True)
    return xc * lax.rsqrt(var + LN_EPS) * g + b


def _shift_rows(x, d, carry):
    rolled = pltpu.roll(x, d, 0)
    head = rolled[:SUBLANES]
    rows = _iota(head.shape, 0)
    for r in range(d):
        src = SUBLANES - d + r
        head = jnp.where(rows == r, carry[src:src + 1, :], head)
    return jnp.concatenate([head, rolled[SUBLANES:]], axis=0)


def _cumsum_rows(x):
    rows = _iota(x.shape, 0)
    d = 1
    while d < x.shape[0]:
        x = x + jnp.where(rows >= d, pltpu.roll(x, d, 0), 0.0)
        d *= 2
    return x


def _linear_scan_rows(a, u, h_in):
    in_group = _iota(a.shape, 0) % SUBLANES
    d = 1
    while d < SUBLANES:
        keep = in_group >= d
        u = u + a * jnp.where(keep, pltpu.roll(u, d, 0), 0.0)
        a = a * jnp.where(keep, pltpu.roll(a, d, 0), 1.0)
        d *= 2
    groups = []
    for g in range(a.shape[0] // SUBLANES):
        rows = slice(g * SUBLANES, (g + 1) * SUBLANES)
        h = u[rows] + a[rows] * h_in
        h_in = h[SUBLANES - 1:SUBLANES, :]
        groups.append(h)
    return jnp.concatenate(groups, axis=0)


def _tile_lanes(x, n):
    return jnp.concatenate([x] * n, axis=1)


def _head_mask(shape, lane_dim, head):
    lane = _iota(shape, lane_dim)
    return (lane >= head * HEAD_DIM) & (lane < (head + 1) * HEAD_DIM)


def _stack_heads(x):
    return jnp.concatenate(
        [jnp.where(_head_mask(x.shape, 1, h), x, jnp.zeros_like(x)) for h in range(HEADS)], axis=0)


def _head_sum(x, ones_bd):
    return _mdot(_dot, _split(x, 2), [ones_bd])


class RowKernelPart(NamedTuple):
    body: Callable
    operands: list
    operand_specs: list
    out_specs: list
    out_shapes: list
    scratch_shapes: list


def _run_row_kernels(h16, parts, name):
    bsz, lp, d = h16.shape
    n_in = sum(len(p.operands) for p in parts)
    n_out = sum(len(p.out_specs) for p in parts)

    def body(h_ref, *refs):
        ins, outs, scratch = list(refs[:n_in]), list(refs[n_in:n_in + n_out]), list(refs[n_in + n_out:])
        for p in parts:
            take = lambda pool, n: [pool.pop(0) for _ in range(n)]
            p.body(h_ref, *take(ins, len(p.operands)), *take(outs, len(p.out_specs)),
                   *take(scratch, len(p.scratch_shapes)))

    return pl.pallas_call(
        body,
        grid=(bsz, lp // ROW_TILE),
        in_specs=[_rows(ROW_TILE, d)] + [s for p in parts for s in p.operand_specs],
        out_specs=[s for p in parts for s in p.out_specs],
        out_shape=[s for p in parts for s in p.out_shapes],
        scratch_shapes=[s for p in parts for s in p.scratch_shapes],
        compiler_params=_cparams("parallel", "arbitrary"),
        name=name,
    )(h16, *[t for p in parts for t in p.operands])


def _ln_in_kernel(x_ref, meta_ref, g_ref, b_ref, o32_ref, o16_ref, carry_ref, *, t_real):
    t = pl.program_id(1)

    @pl.when(t == 0)
    def _():
        carry_ref[...] = meta_ref[...]

    x = x_ref[...]
    tile = x.shape[0]
    tokens = jnp.concatenate([carry_ref[...], x[:tile - N_META]], axis=0)
    carry_ref[...] = x[tile - N_META:]
    tokens = jnp.where(_iota(tokens.shape, 0) + t * tile < t_real, tokens, 0.0)
    y = _layer_norm(tokens, g_ref[...], b_ref[...])
    o32_ref[...] = y
    o16_ref[...] = y.astype(BF16)


def _ln_in(x, meta, g, b, lp):
    bsz, seq, d = x.shape
    last = seq // ROW_TILE - 1
    return pl.pallas_call(
        functools.partial(_ln_in_kernel, t_real=N_META + seq),
        grid=(bsz, lp // ROW_TILE),
        in_specs=[pl.BlockSpec((None, ROW_TILE, d), lambda b, t: (b, jnp.minimum(t, last), 0)),
                  _whole((N_META, d)), _whole((1, d)), _whole((1, d))],
        out_specs=[_rows(ROW_TILE, d), _rows(ROW_TILE, d)],
        out_shape=[jax.ShapeDtypeStruct((bsz, lp, d), F32), jax.ShapeDtypeStruct((bsz, lp, d), BF16)],
        scratch_shapes=[pltpu.VMEM((N_META, d), F32)],
        compiler_params=_cparams("parallel", "arbitrary"),
        name="ln_in",
    )(x, meta.astype(x.dtype), g.reshape(1, d), b.reshape(1, d))


def _lru_kernel(h_ref, w_ref, cw_ref, cb_ref, wrg_ref, brg_ref, wig_ref, big_ref, lam_ref,
                o_ref, xcarry_ref, hcarry_ref):
    @pl.when(pl.program_id(1) == 0)
    def _():
        xcarry_ref[...] = jnp.zeros_like(xcarry_ref)
        hcarry_ref[...] = jnp.zeros_like(hcarry_ref)

    yx = _dot(h_ref[...], w_ref[...])
    y = yx[:, :LRU_WIDTH]
    x = yx[:, LRU_WIDTH:]
    xcarry = xcarry_ref[...]
    xc = cw_ref[LRU_CONV - 1:LRU_CONV, :] * x + cb_ref[...]
    for d in range(1, LRU_CONV):
        k = LRU_CONV - 1 - d
        xc = xc + cw_ref[k:k + 1, :] * _shift_rows(x, d, xcarry)
    xcarry_ref[...] = x[ROW_TILE - SUBLANES:, :]

    xc16 = xc.astype(BF16)
    r = _sigmoid(_dot(xc16, wrg_ref[...]) + brg_ref[...])
    i = _sigmoid(_dot(xc16, wig_ref[...]) + big_ref[...])
    log_a = (-LRU_C) * r * _softplus(-lam_ref[...])
    a = jnp.exp(log_a)
    th = jnp.tanh(log_a)
    u = jnp.sqrt(-2.0 * th / (1.0 - th)) * (i * xc)
    h = _linear_scan_rows(a, u, hcarry_ref[SUBLANES - 1:SUBLANES, :])
    hcarry_ref[...] = h[ROW_TILE - SUBLANES:, :]
    o_ref[...] = (_gelu_tanh(y) * h).astype(BF16)


def _block_diag(w):
    n, d, e = w.shape
    eye = jnp.eye(n, dtype=w.dtype)
    return (eye[:, None, :, None] * w[:, :, None, :]).reshape(n * d, n * e)


def _lru_branch(h16, w_yx, conv_w, conv_b, w_rg, b_rg, w_ig, b_ig, lam):
    bsz, lp, d = h16.shape
    row = lambda v: v.reshape(1, LRU_WIDTH)
    operands = [w_yx.astype(BF16), conv_w, row(conv_b), _block_diag(w_rg).astype(BF16), row(b_rg),
                _block_diag(w_ig).astype(BF16), row(b_ig), row(lam)]
    return RowKernelPart(
        _lru_kernel, operands, [_whole(t.shape) for t in operands],
        [_rows(ROW_TILE, LRU_WIDTH)], [jax.ShapeDtypeStruct((bsz, lp, LRU_WIDTH), BF16)],
        [pltpu.VMEM((SUBLANES, LRU_WIDTH), F32), pltpu.VMEM((SUBLANES, LRU_WIDTH), F32)])


def _values_with_ones(v_spread):
    lane = _iota(v_spread.shape, 1)
    return jnp.where(lane % LANES >= HEAD_DIM, 1.0, v_spread).astype(BF16)


def _spread_heads(w):
    k = w.shape[0]
    return jnp.pad(w.reshape(k, HEADS, HEAD_DIM), ((0, 0), (0, 0), (0, LANES - HEAD_DIM))).reshape(k, HEADS * LANES)


def _fox_proj_kernel(h_ref, w_ref, bf_ref, q_ref, k_ref, v_ref, f_ref, fcarry_ref, *, q_scale):
    @pl.when(pl.program_id(1) == 0)
    def _():
        fcarry_ref[...] = jnp.zeros_like(fcarry_ref)

    z = _dot(h_ref[...], w_ref[...])
    q_ref[...] = (z[:, 0:WIDTH] * q_scale).astype(BF16)
    k_ref[...] = z[:, WIDTH:2 * WIDTH].astype(BF16)
    v_ref[...] = _values_with_ones(z[:, 2 * WIDTH:2 * WIDTH + HEADS * LANES])
    log_f = -_softplus(-(z[:, 2 * WIDTH + HEADS * LANES:] + bf_ref[...]))
    f = _cumsum_rows(log_f) + fcarry_ref[SUBLANES - 1:SUBLANES, :]
    fcarry_ref[...] = f[ROW_TILE - SUBLANES:, :]
    f_ref[...] = f


def _fox_proj(h16, w_qkvf, b_f, q_scale):
    bsz, lp, d = h16.shape
    bf = jnp.zeros((1, LANES), F32).at[0, :HEADS].set(b_f)
    act = lambda w: jax.ShapeDtypeStruct((bsz, lp, w), BF16)
    return RowKernelPart(
        functools.partial(_fox_proj_kernel, q_scale=q_scale),
        [w_qkvf.astype(BF16), bf], [_whole(w_qkvf.shape), _whole((1, LANES))],
        [_rows(ROW_TILE, WIDTH), _rows(ROW_TILE, WIDTH), _rows(ROW_TILE, HEADS * LANES), _rows(ROW_TILE, LANES)],
        [act(WIDTH), act(WIDTH), act(HEADS * LANES), jax.ShapeDtypeStruct((bsz, lp, LANES), F32)],
        [pltpu.VMEM((SUBLANES, LANES), F32)])


def _rms_norm(x, g):
    return x * lax.rsqrt(jnp.mean(x * x, -1, keepdims=True) + RMS_EPS) * g


def _mla_proj_kernel(h_ref, win_ref, qg_ref, kvg_ref, wq_ref, wkv_ref, cos_ref, sin_ref,
                     qn_ref, qr_ref, k01_ref, k23_ref, v_ref, *, q_scale):
    z = _dot(h_ref[...], win_ref[...])
    c_q = z[:, 0:MLA_Q_RANK]
    c_kv = z[:, MLA_Q_RANK:MLA_Q_RANK + MLA_KV_RANK]
    k_r = z[:, MLA_Q_RANK + MLA_KV_RANK:MLA_Q_RANK + MLA_KV_RANK + LANES]
    k_r_swapped = z[:, MLA_Q_RANK + MLA_KV_RANK + LANES:]
    cos = cos_ref[...]
    sin = sin_ref[...]

    q = _dot(_rms_norm(c_q, qg_ref[...]).astype(BF16), wq_ref[...])
    qn_ref[...] = (q[:, 0:WIDTH] * q_scale).astype(BF16)
    rope_w = HEADS * LANES
    for h in range(HEADS):
        plain = q[:, WIDTH + h * LANES:WIDTH + (h + 1) * LANES]
        swapped = q[:, WIDTH + rope_w + h * LANES:WIDTH + rope_w + (h + 1) * LANES]
        qr_ref[:, h * LANES:(h + 1) * LANES] = ((plain * cos + swapped * sin) * q_scale).astype(BF16)

    kv = _dot(_rms_norm(c_kv, kvg_ref[...]).astype(BF16), wkv_ref[...])
    k_rope = (k_r * cos + k_r_swapped * sin).astype(BF16)
    for g, k_ref in enumerate((k01_ref, k23_ref)):
        k_ref[:, 0:LANES] = kv[:, g * LANES:(g + 1) * LANES].astype(BF16)
        k_ref[:, LANES:] = k_rope
    v_ref[...] = _values_with_ones(kv[:, WIDTH:])


def _swap_halves(w):
    half = w.shape[-1] // 2
    return jnp.concatenate([w[..., half:], w[..., :half]], axis=-1)


def _pad_cols(w, width):
    return jnp.pad(w, ((0, 0), (0, width - w.shape[1])))


def _take_cols_kernel(a_ref, b_ref, o_ref, *, shift):
    both = jnp.concatenate([a_ref[...], b_ref[...]], axis=1)
    o_ref[...] = both[:, shift:shift + o_ref.shape[1]].astype(BF16)


def _take_cols(w, layer, start, width, block):
    rows = w.shape[1]
    first, shift = divmod(start, block)
    assert (first + width // block) * block < w.shape[2]
    return pl.pallas_call(
        functools.partial(_take_cols_kernel, shift=shift),
        grid=(width // block,),
        in_specs=[pl.BlockSpec((None, rows, block), lambda j: (layer, 0, j + first)),
                  pl.BlockSpec((None, rows, block), lambda j: (layer, 0, j + first + 1))],
        out_specs=pl.BlockSpec((rows, block), lambda j: (0, j)),
        out_shape=jax.ShapeDtypeStruct((rows, width), BF16),
        compiler_params=_cparams("parallel"),
        name="take_cols",
    )(w, w)


def _mla_proj(h16, w_cq, w_ckv, w_kr, q_norm_g, kv_norm_g, w_uq, w_ukv, cos_t, sin_t, q_scale):
    bsz, lp, d = h16.shape
    w_in = jnp.concatenate([w_cq, w_ckv, _pad_cols(w_kr, LANES), _pad_cols(_swap_halves(w_kr), LANES)], axis=1)
    uq = w_uq.reshape(MLA_Q_RANK, HEADS, MLA_NOPE + MLA_ROPE)
    q_nope = uq[:, :, :MLA_NOPE].reshape(MLA_Q_RANK, WIDTH)
    q_rope = uq[:, :, MLA_NOPE:]
    lane_pad = lambda r: jnp.pad(r, ((0, 0), (0, 0), (0, LANES - MLA_ROPE))).reshape(MLA_Q_RANK, HEADS * LANES)
    wq = jnp.concatenate([q_nope, lane_pad(q_rope), lane_pad(_swap_halves(q_rope))], axis=1)
    ukv = w_ukv.reshape(MLA_KV_RANK, HEADS, MLA_NOPE + HEAD_DIM)
    wkv = jnp.concatenate([ukv[:, :, :MLA_NOPE].reshape(MLA_KV_RANK, WIDTH),
                           _spread_heads(ukv[:, :, MLA_NOPE:].reshape(MLA_KV_RANK, WIDTH))], axis=1)
    table = pl.BlockSpec((ROW_TILE, LANES), lambda b, t: (t, 0))
    act = lambda w: jax.ShapeDtypeStruct((bsz, lp, w), BF16)
    return RowKernelPart(
        functools.partial(_mla_proj_kernel, q_scale=q_scale),
        [w_in.astype(BF16), q_norm_g.reshape(1, -1), kv_norm_g.reshape(1, -1), wq.astype(BF16), wkv.astype(BF16),
         cos_t, sin_t],
        [_whole(w_in.shape), _whole((1, MLA_Q_RANK)), _whole((1, MLA_KV_RANK)), _whole(wq.shape), _whole(wkv.shape),
         table, table],
        [_rows(ROW_TILE, WIDTH), _rows(ROW_TILE, HEADS * LANES), _rows(ROW_TILE, WIDTH), _rows(ROW_TILE, WIDTH),
         _rows(ROW_TILE, HEADS * LANES)],
        [act(WIDTH), act(HEADS * LANES), act(WIDTH), act(WIDTH), act(HEADS * LANES)],
        [])


def _rotary_tables(lp):
    inv = 1.0 / (ROPE_BASE ** (jnp.arange(0, MLA_ROPE, 2, dtype=F32) / MLA_ROPE))
    ang = jnp.arange(lp, dtype=F32)[:, None] * inv[None, :]
    cos, sin = jnp.cos(ang), jnp.sin(ang)
    pad = jnp.zeros((lp, LANES - MLA_ROPE), F32)
    return (jnp.concatenate([cos, cos, pad], axis=1), jnp.concatenate([-sin, sin, pad], axis=1))


def _attn_kernel(*refs, has_rope, has_bias):
    it = iter(refs)
    q_ref = next(it)
    qr_ref = next(it) if has_rope else None
    k_refs = [next(it) for _ in range(2 if has_rope else 1)]
    v_ref = next(it)
    f_ref = next(it) if has_bias else None
    o_ref = next(it)
    m_ref, acc_ref = next(it), next(it)

    tq = q_ref.shape[0]
    qi = pl.program_id(1)
    heads_per_group = HEADS // len(k_refs)
    q_stacks = []
    for g in range(len(k_refs)):
        parts = []
        for h in range(g * heads_per_group, (g + 1) * heads_per_group):
            if has_rope:
                qg = q_ref[:, g * LANES:(g + 1) * LANES]
                qh = jnp.where(_head_mask(qg.shape, 1, h % heads_per_group), qg, jnp.zeros_like(qg))
                qh = jnp.concatenate([qh, qr_ref[:, h * LANES:(h + 1) * LANES]], axis=1)
            else:
                q = q_ref[...]
                qh = jnp.where(_head_mask(q.shape, 1, h), q, jnp.zeros_like(q))
            parts.append(qh)
        q_stacks.append(jnp.concatenate(parts, axis=0))

    m_ref[...] = jnp.full(m_ref.shape, NEG_BIG, F32)
    acc_ref[...] = jnp.zeros_like(acc_ref)
    if has_bias:
        f_q0 = f_ref[:, pl.ds(pl.multiple_of(qi * tq, tq), LANES)][:, 0:1]

    def block(start, tk, masked):
        scores = [_dot_nt(qs, k_ref[pl.ds(start, tk), :]) for qs, k_ref in zip(q_stacks, k_refs)]
        if has_bias:
            bias = (f_q0 - f_ref[:, pl.ds(start, tk)]) * LOG2E
        if masked:
            keep = _iota((tq, tk), 0) >= _iota((tq, tk), 1)
        for h in range(HEADS):
            rows = slice(h * tq, (h + 1) * tq)
            in_group = h % heads_per_group
            sh = scores[h // heads_per_group][in_group * tq:(in_group + 1) * tq]
            if has_bias:
                sh = sh + bias[h:h + 1, :]
            if masked:
                sh = jnp.where(keep, sh, NEG_BIG)
            m_old = m_ref[rows]
            m_new = jnp.maximum(m_old, jnp.max(sh, axis=1, keepdims=True))
            p = jnp.exp2(sh - _tile_lanes(m_new, tk // LANES))
            alpha = jnp.exp2(m_old - m_new)
            m_ref[rows] = m_new
            v_ones = v_ref[pl.ds(start, tk), h * LANES:(h + 1) * LANES]
            acc_ref[rows] = acc_ref[rows] * alpha + _dot(p.astype(BF16), v_ones)

    per = ATTN_KEYS // tq
    step_keys = ATTN_UNROLL * ATTN_KEYS
    n_steps = (qi * tq) // step_keys

    def body(j, carry):
        base = pl.multiple_of(j * step_keys, step_keys)
        for u in range(ATTN_UNROLL):
            block(base + u * ATTN_KEYS, ATTN_KEYS, False)
        return carry

    lax.fori_loop(0, n_steps, body, 0)
    left = qi - n_steps * (step_keys // tq)
    rest = pl.multiple_of(n_steps * step_keys, step_keys)
    for u in range(ATTN_UNROLL - 1):
        @pl.when(left >= (u + 1) * per)
        def _():
            block(rest + u * ATTN_KEYS, ATTN_KEYS, False)
    tail = left % per
    width, pos = per // 2, qi - tail
    while width >= 1:
        @pl.when(tail & width != 0)
        def _(width=width, pos=pos):
            block(pl.multiple_of(pos * tq, tq), width * tq, False)
        pos = pos + (tail & width)
        width //= 2

    block(pl.multiple_of(qi * tq, tq), tq, True)

    first_half = _iota((tq, LANES), 1) < HEAD_DIM
    heads_out = []
    for h in range(HEADS):
        acc = acc_ref[h * tq:(h + 1) * tq]
        heads_out.append(acc / pltpu.roll(acc, HEAD_DIM, 1))
    for g in range(HEADS // 2):
        pair = jnp.where(first_half, heads_out[2 * g], pltpu.roll(heads_out[2 * g + 1], HEAD_DIM, 1))
        o_ref[:, g * LANES:(g + 1) * LANES] = pair.astype(BF16)


def _attention(q, keys, v_ones, q_rope=None, f_rows=None):
    bsz, lp, _ = q.shape
    tq = ROW_TILE
    resident = lambda w: pl.BlockSpec((None, lp, w), lambda b, t: (b, 0, 0))
    in_specs = [_rows(tq, WIDTH)]
    args = [q]
    if q_rope is not None:
        in_specs.append(_rows(tq, HEADS * LANES))
        args.append(q_rope)
    in_specs += [resident(WIDTH)] * len(keys) + [resident(HEADS * LANES)]
    args += [*keys, v_ones]
    if f_rows is not None:
        in_specs.append(pl.BlockSpec((None, SUBLANES, lp), lambda b, t: (b, 0, 0)))
        args.append(f_rows)
    kern = functools.partial(_attn_kernel, has_rope=q_rope is not None,
                             has_bias=f_rows is not None)
    return pl.pallas_call(
        kern,
        grid=(bsz, lp // tq),
        in_specs=in_specs,
        out_specs=_rows(tq, WIDTH),
        out_shape=jax.ShapeDtypeStruct((bsz, lp, WIDTH), BF16),
        scratch_shapes=[pltpu.VMEM((HEADS * tq, LANES), F32), pltpu.VMEM((HEADS * tq, LANES), F32)],
        compiler_params=_cparams("parallel", "arbitrary"),
        name="attn_bias" if f_rows is not None else "attn_rope",
    )(*args)


def _bf(x):
    return x.astype(BF16)


def _stack_bf(x):
    return _bf(_stack_heads(x))


def _unit_lower_inverse(ms, sub_blk, eye):
    n = HEADS * RWKV_CHUNK
    eye_bd = (_iota((n, n), 0) == _iota((n, n), 1)).astype(F32)
    md = [jnp.where(sub_blk, m, 0.0) for m in ms]
    mo = [_stack_bf(m - d) for m, d in zip(ms, md)]
    x = [eye + d for d in md]
    p = md
    p_bd = [_stack_heads(t) for t in p]
    for _ in range(int(math.log2(RWKV_SUB)) - 1):
        p = [_dot(_bf(t), _bf(s)) for t, s in zip(p, p_bd)]
        p_bd = [_stack_heads(t) for t in p]
        x = [_dot(_bf(a), _bf(eye_bd + s)) for a, s in zip(x, p_bd)]
    x_bd = [_stack_bf(a) for a in x]
    pm = [_dot(_bf(a), o) for a, o in zip(x, mo)]
    pm2 = [_dot(_bf(t), _stack_bf(t)) for t in pm]
    pm3 = [_dot(_bf(a), _stack_bf(b)) for a, b in zip(pm, pm2)]
    return [_dot(_bf(eye + a + b + c), s) for a, b, c, s in zip(pm, pm2, pm3, x_bd)]


def _rwkv_kernel(h_ref, w_ref, mu_ref, w0_ref, w2_ref, a0_ref, a2_ref, g2_ref, kk_ref, ka_ref,
                 rk_ref, gng_ref, gnb_ref, ones_ref, o_ref, pcarry_ref, state_ref, y_ref):
    @pl.when(pl.program_id(0) == 0)
    def _():
        pcarry_ref[...] = jnp.zeros_like(pcarry_ref)
        state_ref[...] = jnp.zeros_like(state_ref)

    bsz, tile, _ = h_ref.shape
    ones_bd = ones_ref[...]
    wide = (RWKV_CHUNK, HEADS * RWKV_CHUNK)
    t_idx, s_idx = _iota(wide, 0), _iota(wide, 1) % RWKV_CHUNK
    strict = t_idx > s_idx
    incl = t_idx >= s_idx
    sub_blk = (t_idx // RWKV_SUB) == (s_idx // RWKV_SUB)
    eye = (t_idx == s_idx).astype(F32)
    diag = _iota((WIDTH, WIDTH), 0) == _iota((WIDTH, WIDTH), 1)
    mid = RWKV_CHUNK // 2 - 1
    chunk_rows = [slice(c * RWKV_CHUNK, (c + 1) * RWKV_CHUNK) for c in range(tile // RWKV_CHUNK)]

    seqs = []
    ar_t, b_t, k_t, a_0, r_0, b_e, k_e, v_s, w_end = ([] for _ in range(9))
    for b in range(bsz):
        p_in = _dot(h_ref[b], w_ref[...])
        prev = _shift_rows(p_in, 1, pcarry_ref[b])
        pcarry_ref[b] = p_in[tile - SUBLANES:, :]
        p = p_in + (prev - p_in) * mu_ref[...]
        r = p[:, 0:WIDTH]
        k = p[:, WIDTH:2 * WIDTH]
        v = p[:, 2 * WIDTH:3 * WIDTH]
        lora = p[:, 3 * WIDTH:]
        z = w0_ref[...] + _dot(jnp.tanh(lora).astype(BF16), w2_ref[...])
        log_w = (-RWKV_DECAY_SCALE) * _sigmoid(z)
        a_gate = _sigmoid(a0_ref[...] + _dot(lora.astype(BF16), a2_ref[...]))
        gate = _dot(_sigmoid(lora).astype(BF16), g2_ref[...])
        kk = k * kk_ref[...]
        kk = kk * lax.rsqrt(jnp.maximum(_head_sum(kk * kk, ones_bd), 1e-24))
        kf = k * (1.0 + (a_gate - 1.0) * ka_ref[...])
        a_vec = -kk
        b_vec = kk * a_gate
        seqs.append((r, kf, v, gate))
        for rows in chunk_rows:
            lw = log_w[rows]
            g_in = _cumsum_rows(lw)
            g_ex = g_in - lw
            g_mid = g_in[mid:mid + 1, :]
            g_end = g_in[RWKV_CHUNK - 1:RWKV_CHUNK, :]
            decay_out = jnp.exp(g_mid - g_in)
            to_end = jnp.exp(g_end - g_in)
            a_c, b_c, k_c, r_c = a_vec[rows], b_vec[rows], kf[rows], r[rows]
            ar_t.append(_bf(jnp.concatenate([a_c * jnp.exp(g_ex - g_mid), r_c * jnp.exp(g_in - g_mid)], axis=0)))
            b_t.append(_stack_bf(b_c * decay_out))
            k_t.append(_stack_bf(k_c * decay_out))
            a_0.append(_stack_bf(a_c * jnp.exp(g_ex)))
            r_0.append(r_c * jnp.exp(g_in))
            b_e.append(_stack_bf(b_c * to_end))
            k_e.append(_stack_bf(k_c * to_end))
            v_s.append(_stack_bf(v[rows]))
            w_end.append(jnp.exp(g_end))

    each = lambda f, *lists: [f(*args) for args in zip(*lists)]
    top, bottom = slice(0, RWKV_CHUNK), slice(RWKV_CHUNK, 2 * RWKV_CHUNK)
    gram_b = each(_dot_nt, ar_t, b_t)
    gram_k = each(_dot_nt, ar_t, k_t)
    m_ab = each(lambda g: jnp.where(strict, g[top], 0.0), gram_b)
    m_ak = each(lambda g: _bf(jnp.where(strict, g[top], 0.0)), gram_k)
    m_rb = each(lambda g: _bf(jnp.where(incl, g[bottom], 0.0)), gram_b)
    m_rk = each(lambda g: _bf(jnp.where(incl, g[bottom], 0.0)), gram_k)
    t_inv = each(_bf, _unit_lower_inverse(m_ab, sub_blk, eye))
    w_a = each(lambda t, a: _stack_bf(_dot(t, a)), t_inv, a_0)
    m_v = each(lambda m, x: _stack_bf(_dot(m, x)), m_ak, v_s)
    u_v = each(lambda t, x: _stack_bf(_dot(t, x)), t_inv, m_v)
    r_eff = each(lambda r0, m, x: r0 + _dot(m, x), r_0, m_rb, w_a)
    y_v = each(lambda mb, u, mk, x: _dot(mb, u) + _dot(mk, x), m_rb, u_v, m_rk, v_s)
    g_mat = each(lambda w, b, x: jnp.where(diag, w, 0.0) + _dot_tn(b, x), w_end, b_e, w_a)
    d_mat = each(lambda b, u, kx, x: _dot_tn(b, u) + _dot_tn(kx, x), b_e, u_v, k_e, v_s)

    for c, rows in enumerate(chunk_rows):
        for b in range(bsz):
            i = b * len(chunk_rows) + c
            state = _bf(state_ref[b])
            y_ref[b, rows, :] = _dot(_bf(r_eff[i]), state) + y_v[i]
            state_ref[b] = _dot(_bf(g_mat[i]), state) + d_mat[i]

    inv_n = 1.0 / HEAD_DIM
    for b, (r, kf, v, gate) in enumerate(seqs):
        y = y_ref[b]
        y_mu = _head_sum(y, ones_bd) * inv_n
        yc = y - y_mu
        y_var = _head_sum(yc * yc, ones_bd) * inv_n
        y = yc * lax.rsqrt(y_var + RWKV_GN_EPS) * gng_ref[...] + gnb_ref[...]
        y = y + _head_sum(r * kf * rk_ref[...], ones_bd) * v
        o_ref[b] = (y * gate).astype(BF16)


def _rwkv_branch(h16, w_p, mu, w0, w2, a0, a2, g2, k_k, k_a, r_k, gn_g, gn_b):
    bsz, lp, d = h16.shape
    row = lambda t: t.reshape(1, -1)
    lora_rows = lambda w, lo: jnp.zeros((RWKV_LORA, WIDTH), F32).at[lo:lo + w.shape[0]].set(w).astype(BF16)
    ones_bd = _block_diag(jnp.ones((HEADS, HEAD_DIM, HEAD_DIM), F32)).astype(BF16)
    a2_at = w2.shape[0]
    g2_at = a2_at + a2.shape[0]
    assert g2_at + g2.shape[0] == RWKV_LORA
    params = [w_p.astype(BF16), row(mu), row(w0), lora_rows(w2, 0), row(a0), lora_rows(a2, a2_at),
              lora_rows(g2, g2_at), row(k_k), row(k_a), row(r_k), row(gn_g), row(gn_b), ones_bd]
    all_seqs = lambda w: pl.BlockSpec((bsz, ROW_TILE, w), lambda t: (0, t, 0))
    return pl.pallas_call(
        _rwkv_kernel,
        grid=(lp // ROW_TILE,),
        in_specs=[all_seqs(d)] + [_whole(t.shape) for t in params],
        out_specs=all_seqs(WIDTH),
        out_shape=jax.ShapeDtypeStruct((bsz, lp, WIDTH), BF16),
        scratch_shapes=[pltpu.VMEM((bsz, SUBLANES, RWKV_IN), F32), pltpu.VMEM((bsz, WIDTH, WIDTH), F32),
                        pltpu.VMEM((bsz, ROW_TILE, WIDTH), F32)],
        compiler_params=_cparams("arbitrary"),
        name="rwkv",
    )(h16, *params)


def _merge_kernel(h32_ref, h16_ref, oa_ref, ob_ref, oc_ref, od_ref, wg_ref, wb_ref, wo_ref, g_ref, b_ref,
                  o32_ref, o16_ref, *, alpha):
    h16 = h16_ref[...]
    acc = None
    for n, o_ref in enumerate((oa_ref, ob_ref, oc_ref, od_ref)):
        gate = _sigmoid(_dot(h16, wg_ref[:, n * D_MODEL:(n + 1) * D_MODEL]))
        term = gate * _dot(o_ref[...], wb_ref[n])
        acc = term if acc is None else acc + term
    mixed = _dot(acc.astype(BF16), wo_ref[...])
    y = _layer_norm(alpha * h32_ref[...] + mixed, g_ref[...], b_ref[...])
    o32_ref[...] = y
    o16_ref[...] = y.astype(BF16)


def _merge(h32, h16, branches, w_gate, w_branch, w_out, ln_g, ln_b, alpha):
    bsz, lp, d = h32.shape
    return pl.pallas_call(
        functools.partial(_merge_kernel, alpha=alpha),
        grid=(bsz, lp // ROW_TILE),
        in_specs=[_rows(ROW_TILE, d), _rows(ROW_TILE, d)] + [_rows(ROW_TILE, WIDTH)] * N_BRANCH
        + [_whole((d, N_BRANCH * d)), _whole((N_BRANCH, WIDTH, d)), _whole((d, d)), _whole((1, d)), _whole((1, d))],
        out_specs=[_rows(ROW_TILE, d), _rows(ROW_TILE, d)],
        out_shape=[jax.ShapeDtypeStruct(h32.shape, F32), jax.ShapeDtypeStruct(h32.shape, BF16)],
        compiler_params=_cparams("parallel", "parallel"),
        name="merge",
    )(h32, h16, *branches, w_gate.astype(BF16), w_branch.astype(BF16), w_out.astype(BF16),
      ln_g.reshape(1, d), ln_b.reshape(1, d))


def _ffn_kernel(h32_ref, h16_ref, wu_ref, cw_ref, cb_ref, wd_ref, g_ref, b_ref, *rest, alpha, last_layer):
    if last_layer:
        out_ref, carry_ref, act_ref, prev_ref = rest
    else:
        o32_ref, o16_ref, carry_ref, act_ref = rest

    @pl.when(pl.program_id(1) == 0)
    def _():
        carry_ref[...] = jnp.zeros_like(carry_ref)
        if last_layer:
            prev_ref[...] = jnp.zeros_like(prev_ref)

    h16 = h16_ref[...]

    def conv_cols(lo):
        cols = slice(lo, lo + FFN_CHUNK)
        x = _dot(h16, wu_ref[:, cols])
        carry = carry_ref[:, cols]
        out = cw_ref[FFN_CONV - 1:FFN_CONV, cols] * x + cb_ref[:, cols]
        for d in range(1, FFN_CONV):
            k = FFN_CONV - 1 - d
            out = out + cw_ref[k:k + 1, cols] * _shift_rows(x, d, carry)
        carry_ref[:, cols] = x[ROW_TILE - SUBLANES:, :]
        return out

    for j in range(D_FF // FFN_CHUNK):
        gate = conv_cols(j * FFN_CHUNK)
        val = conv_cols(D_FF + j * FFN_CHUNK)
        act_ref[:, j * FFN_CHUNK:(j + 1) * FFN_CHUNK] = (gate * _sigmoid(gate) * val).astype(BF16)
    y = _layer_norm(alpha * h32_ref[...] + _dot(act_ref[...], wd_ref[...]), g_ref[...], b_ref[...])
    if last_layer:
        out_ref[...] = jnp.concatenate([prev_ref[N_META:, :], y[:N_META]], axis=0)
        prev_ref[...] = y
    else:
        o32_ref[...] = y
        o16_ref[...] = y.astype(BF16)


def _ffn(h32, h16, w_up, conv_w, conv_b, w_down, ln_g, ln_b, alpha, last_layer):
    bsz, lp, d = h32.shape
    params = [w_up.astype(BF16), jnp.pad(conv_w, ((0, SUBLANES - FFN_CONV), (0, 0))), conv_b.reshape(1, -1),
              w_down.astype(BF16), ln_g.reshape(1, d), ln_b.reshape(1, d)]
    scratch = [pltpu.VMEM((SUBLANES, 2 * D_FF), F32), pltpu.VMEM((ROW_TILE, D_FF), BF16)]
    if last_layer:
        out_specs = pl.BlockSpec((None, ROW_TILE, d), lambda b, t: (b, jnp.maximum(t - 1, 0), 0))
        out_shape = jax.ShapeDtypeStruct((bsz, lp - ROW_TILE, d), F32)
        scratch.append(pltpu.VMEM((ROW_TILE, d), F32))
    else:
        out_specs = [_rows(ROW_TILE, d), _rows(ROW_TILE, d)]
        out_shape = [jax.ShapeDtypeStruct(h32.shape, F32), jax.ShapeDtypeStruct(h32.shape, BF16)]
    return pl.pallas_call(
        functools.partial(_ffn_kernel, alpha=alpha, last_layer=last_layer),
        grid=(bsz, lp // ROW_TILE),
        in_specs=[_rows(ROW_TILE, d), _rows(ROW_TILE, d)] + [_whole(t.shape) for t in params],
        out_specs=out_specs,
        out_shape=out_shape,
        scratch_shapes=scratch,
        compiler_params=_cparams("parallel", "arbitrary"),
        name="ffn_last" if last_layer else "ffn",
    )(h32, h16, *params)


def kernel(x, meta_tokens, ln_in_g, ln_in_b, w_in, w_branch, w_out, ln_mix_g, ln_mix_b, lru_conv_w, lru_conv_b, lru_w_rg, lru_b_rg, lru_w_ig, lru_b_ig, lru_lambda, fox_b_f, mla_q_norm_g, mla_kv_norm_g, mla_w_uq, mla_w_ukv, rwkv_mu, rwkv_w0, rwkv_w2, rwkv_a0, rwkv_a2, rwkv_g2, rwkv_k_k, rwkv_k_a, rwkv_r_k, rwkv_gn_g, rwkv_gn_b, ffn_w_up, ffn_conv_w, ffn_conv_b, ffn_w_down, ln_ffn_g, ln_ffn_b):
    bsz, seq, d = x.shape
    depth = w_in.shape[0]
    alpha = (2.0 * depth) ** 0.25
    assert seq % ROW_TILE == 0 and meta_tokens.shape == (N_META, d)
    lp = seq + ROW_TILE
    h32, h16 = _ln_in(x, meta_tokens, ln_in_g, ln_in_b, lp)
    cos_t, sin_t = _rotary_tables(lp)

    widths = (LRU_WIDTH, LRU_WIDTH, WIDTH, WIDTH, WIDTH, HEADS, MLA_Q_RANK, MLA_KV_RANK, MLA_ROPE,
              RWKV_IN, N_BRANCH * D_MODEL)
    offs = np.concatenate([[0], np.cumsum(widths)])
    col = lambda l, i, j=None: w_in[l, :, int(offs[i]):int(offs[(i if j is None else j) + 1])]

    for l in range(depth):
        w_mla = _take_cols(w_in, l,int(offs[6]), 4 * LANES, 4 * LANES)
        w_cq, w_ckv = w_mla[:, :MLA_Q_RANK], w_mla[:, MLA_Q_RANK:MLA_Q_RANK + MLA_KV_RANK]
        w_kr = w_mla[:, MLA_Q_RANK + MLA_KV_RANK:MLA_Q_RANK + MLA_KV_RANK + MLA_ROPE]
        w_tail = _take_cols(w_in, l,int(offs[9]), RWKV_IN + N_BRANCH * D_MODEL, 3 * LANES)
        w_fox = jnp.concatenate([col(l,2, 3), _spread_heads(col(l,4)), _pad_cols(col(l,5), LANES)], axis=1)
        o_a, fq, fk, fv, f_cum, mq, mqr, mk01, mk23, mv = _run_row_kernels(h16, [
            _lru_branch(h16, col(l,0, 1), lru_conv_w[l], lru_conv_b[l], lru_w_rg[l], lru_b_rg[l],
                        lru_w_ig[l], lru_b_ig[l], lru_lambda[l]),
            _fox_proj(h16, w_fox, fox_b_f[l], LOG2E / math.sqrt(HEAD_DIM)),
            _mla_proj(h16, w_cq, w_ckv, w_kr, mla_q_norm_g[l], mla_kv_norm_g[l], mla_w_uq[l], mla_w_ukv[l],
                      cos_t, sin_t, LOG2E / math.sqrt(MLA_NOPE + MLA_ROPE))], "mixer_proj")
        f_rows = jnp.swapaxes(f_cum[:, :, :SUBLANES], 1, 2)
        o_b = _attention(fq, (fk,), fv, f_rows=f_rows)
        o_c = _attention(mq, (mk01, mk23), mv, q_rope=mqr)
        o_d = _rwkv_branch(h16, w_tail[:, :RWKV_IN], rwkv_mu[l], rwkv_w0[l], rwkv_w2[l], rwkv_a0[l], rwkv_a2[l],
                           rwkv_g2[l], rwkv_k_k[l], rwkv_k_a[l], rwkv_r_k[l], rwkv_gn_g[l], rwkv_gn_b[l])
        h32, h16 = _merge(h32, h16, (o_a, o_b, o_c, o_d), w_tail[:, RWKV_IN:], w_branch[l], w_out[l],
                          ln_mix_g[l], ln_mix_b[l], alpha)
        out = _ffn(h32, h16, ffn_w_up[l], ffn_conv_w[l], ffn_conv_b[l], ffn_w_down[l],
                   ln_ffn_g[l], ln_ffn_b[l], alpha, last_layer=l == depth - 1)
        if l < depth - 1:
            h32, h16 = out
    return out
```
